```python
import jax, jax.numpy as jnp
from jax import lax
import numpy as np

D_MODEL = 1024
BATCH = 1
SEQ = 16384
DEPTH = 2
DEC_BATCH = 128
DEC_SEQ = 4
PAST_LEN = 16384
PAGE_SIZE = 128

D_MIX = 1024
N_HEADS = 8
NOPE_DIM = 64
ROPE_DIM = 32
V_DIM = 64
Q_RANK = 256
KV_RANK = 128
ATT_W = N_HEADS * V_DIM
POOL_WINDOWS = (2, 4, 8, 16)
POOL_GROUPS = len(POOL_WINDOWS)
POOL_W = 256
POOL_GW = POOL_W // POOL_GROUPS
POOL_HIST = max(POOL_WINDOWS) - 1
CONV_W = D_MIX - ATT_W - POOL_W
CONV_K = 31
CONV_HIST = CONV_K - 1
IN_COLS = Q_RANK + KV_RANK + ROPE_DIM + POOL_W + 2 * CONV_W + D_MIX
SPLIT_AT = (Q_RANK,
            Q_RANK + KV_RANK,
            Q_RANK + KV_RANK + ROPE_DIM,
            Q_RANK + KV_RANK + ROPE_DIM + POOL_W,
            Q_RANK + KV_RANK + ROPE_DIM + POOL_W + 2 * CONV_W)
ATT_SCALE = (NOPE_DIM + ROPE_DIM) ** -0.5
ROPE_THETA = 10000.0
Q_BLOCK = 128
EPS = 1e-6

kernel_name = 'hymba_mla_pool_conformer_decode_step'


def rms_norm(x, g):
    xf = x.astype(jnp.float32)
    y = xf * lax.rsqrt(jnp.mean(xf * xf, axis=-1, keepdims=True) + EPS)
    return (y * g.astype(jnp.float32)).astype(x.dtype)


def rope_tables(pos):
    inv = ROPE_THETA ** (-jnp.arange(0, ROPE_DIM, 2, dtype=jnp.float32) / ROPE_DIM)
    ang = pos.astype(jnp.float32)[:, None] * inv[None, :]
    return jnp.cos(ang), jnp.sin(ang)


def apply_rope(x, cos, sin):
    xf = x.astype(jnp.float32)
    x1, x2 = jnp.split(xf, 2, axis=-1)
    return jnp.concatenate([x1 * cos - x2 * sin, x1 * sin + x2 * cos], axis=-1).astype(x.dtype)


def mla_scores(q_abs, q_rope, ckv, krope):
    s = (jnp.einsum('bqhr,bkr->bhqk', q_abs, ckv, preferred_element_type=jnp.float32)
         + jnp.einsum('bqhd,bkd->bhqk', q_rope, krope, preferred_element_type=jnp.float32))
    return s * ATT_SCALE


def attend_prompt(q_abs, q_rope, ckv, krope, pos):
    B, L = q_abs.shape[:2]
    nb = L // Q_BLOCK
    qa = q_abs.reshape(B, nb, Q_BLOCK, N_HEADS, KV_RANK).swapaxes(0, 1)
    qr = q_rope.reshape(B, nb, Q_BLOCK, N_HEADS, ROPE_DIM).swapaxes(0, 1)
    qp = pos.reshape(nb, Q_BLOCK)

    def block(args):
        qa_b, qr_b, qp_b = args
        s = mla_scores(qa_b, qr_b, ckv, krope)
        s = jnp.where(pos[None, :] <= qp_b[:, None], s, -jnp.inf)
        p = jax.nn.softmax(s, axis=-1).astype(ckv.dtype)
        return jnp.einsum('bhqk,bkr->bqhr', p, ckv)

    o = lax.map(block, (qa, qr, qp))
    return o.swapaxes(0, 1).reshape(B, L, N_HEADS, KV_RANK)


def attend_sample(q_abs, q_rope, ckv_past, krope_past, ckv_new, krope_new):
    L = q_abs.shape[1]
    s_past = mla_scores(q_abs, q_rope, ckv_past, krope_past)
    s_new = mla_scores(q_abs, q_rope, ckv_new, krope_new)
    causal = jnp.arange(L)[None, :] <= jnp.arange(L)[:, None]
    s_new = jnp.where(causal, s_new, -jnp.inf)
    p = jax.nn.softmax(jnp.concatenate([s_past, s_new], axis=-1), axis=-1).astype(ckv_new.dtype)
    P = s_past.shape[-1]
    return (jnp.einsum('bhqk,bkr->bqhr', p[..., :P], ckv_past)
            + jnp.einsum('bhqk,bkr->bqhr', p[..., P:], ckv_new))


def pool_mix(u_ext, pos, pool_w, pool_scale):
    B, T, C = u_ext.shape
    L = T - POOL_HIST
    uf = u_ext.astype(jnp.float32).reshape(B, T, POOL_GROUPS, POOL_GW)
    cs = jnp.concatenate([jnp.zeros((B, 1, POOL_GROUPS, POOL_GW), jnp.float32),
                          jnp.cumsum(uf, axis=1)], axis=1)
    cur = uf[:, POOL_HIST:]
    end = cs[:, POOL_HIST + 1:]
    outs = []
    for gi, w in enumerate(POOL_WINDOWS):
        start = cs[:, POOL_HIST + 1 - w: POOL_HIST + 1 - w + L, gi]
        cnt = jnp.minimum(pos + 1, w).astype(jnp.float32)[None, :, None]
        outs.append((end[:, :, gi] - start) / cnt - cur[:, :, gi])
    d = jnp.stack(outs, axis=2)
    y = jnp.einsum('blgc,gcd->blgd', d, pool_w.astype(jnp.float32)).reshape(B, L, C)
    return (y * pool_scale.astype(jnp.float32)).astype(u_ext.dtype)


def conv_module(a_ext, dw, db, ln_g, ln_b, pw):
    C = a_ext.shape[-1]
    y = lax.conv_general_dilated(a_ext, dw[:, None, :], window_strides=(1,), padding='VALID',
                                 dimension_numbers=('NWC', 'WIO', 'NWC'),
                                 feature_group_count=C) + db
    yf = y.astype(jnp.float32)
    mu = jnp.mean(yf, axis=-1, keepdims=True)
    var = jnp.mean(jnp.square(yf - mu), axis=-1, keepdims=True)
    yn = (yf - mu) * lax.rsqrt(var + EPS) * ln_g.astype(jnp.float32) + ln_b.astype(jnp.float32)
    return jax.nn.silu(yn).astype(a_ext.dtype) @ pw


def layer(x, pos, p, attend, pool_prev, conv_prev):
    B, L, _ = x.shape
    h = rms_norm(x, p['norm_pre'])
    z = h @ p['w_in']
    c_q, c_kv, k_r, u_pool, u_conv, gate = jnp.split(z, SPLIT_AT, axis=-1)
    cos, sin = rope_tables(pos)
    c_q = rms_norm(c_q, p['q_norm'])
    q = (c_q @ p['w_uq']).reshape(B, L, N_HEADS, NOPE_DIM + ROPE_DIM)
    q_nope = q[..., :NOPE_DIM]
    q_rope = apply_rope(q[..., NOPE_DIM:], cos[:, None, :], sin[:, None, :])
    c_kv = rms_norm(c_kv, p['kv_norm'])
    k_r = apply_rope(k_r, cos, sin)
    q_abs = jnp.einsum('blhd,rhd->blhr', q_nope, p['w_uk'])
    o_lat = attend(q_abs, q_rope, c_kv, k_r)
    o_att = jnp.einsum('blhr,rhd->blhd', o_lat, p['w_uv']).reshape(B, L, ATT_W)
    pool_ext = jnp.concatenate([pool_prev, u_pool], axis=1)
    y_pool = pool_mix(pool_ext, pos, p['pool_w'], p['pool_scale'])
    a = u_conv[..., :CONV_W] * jax.nn.sigmoid(u_conv[..., CONV_W:])
    conv_ext = jnp.concatenate([conv_prev, a], axis=1)
    y_conv = conv_module(conv_ext, p['conv_dw'], p['conv_b'], p['conv_ln_g'], p['conv_ln_b'], p['conv_pw'])
    mixed = jnp.concatenate([o_att, y_pool, y_conv], axis=-1) * jax.nn.silu(gate)
    out = rms_norm(mixed @ p['w_out'], p['norm_post'])
    return (x + out, c_kv, k_r, pool_ext[:, -POOL_HIST:], conv_ext[:, -CONV_HIST:])


def setup_inputs(seed: int = 0) -> dict:
    key = jax.random.key(seed)
    ks = jax.random.split(key, 32)
    f32 = jnp.float32
    n_pages = PAST_LEN // PAGE_SIZE
    n_used = DEC_BATCH * n_pages
    n_phys = n_used + max(1, n_used // 4)

    def nrm(k, shape, scale=1.0):
        return scale * jax.random.normal(k, shape, f32)

    def gain(k, shape):
        return 1.0 + 0.1 * jax.random.normal(k, shape, f32)

    page_table = jax.random.permutation(ks[4], n_phys)[:n_used].reshape(DEC_BATCH, n_pages).astype(jnp.int32)
    return {
        'x_prompt': nrm(ks[0], (BATCH, SEQ, D_MODEL)),
        'x_sample': nrm(ks[1], (DEC_BATCH, DEC_SEQ, D_MODEL)),
        'cache_ckv': nrm(ks[2], (DEPTH, n_phys, PAGE_SIZE, KV_RANK)),
        'cache_krope': nrm(ks[3], (DEPTH, n_phys, PAGE_SIZE, ROPE_DIM)),
        'page_table': page_table,
        'state_pool': nrm(ks[5], (DEPTH, DEC_BATCH, POOL_HIST, POOL_W)),
        'state_conv': nrm(ks[6], (DEPTH, DEC_BATCH, CONV_HIST, CONV_W), 0.5),
        'norm_pre': gain(ks[7], (DEPTH, D_MODEL)),
        'w_in': nrm(ks[8], (DEPTH, D_MODEL, IN_COLS), D_MODEL ** -0.5),
        'q_norm': gain(ks[9], (DEPTH, Q_RANK)),
        'w_uq': nrm(ks[10], (DEPTH, Q_RANK, N_HEADS * (NOPE_DIM + ROPE_DIM)), Q_RANK ** -0.5),
        'kv_norm': gain(ks[11], (DEPTH, KV_RANK)),
        'w_uk': nrm(ks[12], (DEPTH, KV_RANK, N_HEADS, NOPE_DIM), KV_RANK ** -0.5),
        'w_uv': nrm(ks[13], (DEPTH, KV_RANK, N_HEADS, V_DIM), KV_RANK ** -0.5),
        'pool_w': nrm(ks[14], (DEPTH, POOL_GROUPS, POOL_GW, POOL_GW), POOL_GW ** -0.5),
        'pool_scale': gain(ks[15], (DEPTH, POOL_W)),
        'conv_dw': nrm(ks[16], (DEPTH, CONV_K, CONV_W), CONV_K ** -0.5),
        'conv_b': nrm(ks[17], (DEPTH, CONV_W), 0.02),
        'conv_ln_g': gain(ks[18], (DEPTH, CONV_W)),
        'conv_ln_b': nrm(ks[19], (DEPTH, CONV_W), 0.02),
        'conv_pw': nrm(ks[20], (DEPTH, CONV_W, CONV_W), CONV_W ** -0.5),
        'w_out': nrm(ks[21], (DEPTH, D_MIX, D_MODEL), D_MIX ** -0.5),
        'norm_post': gain(ks[22], (DEPTH, D_MODEL)),
    }


def reference(x_prompt, x_sample, cache_ckv, cache_krope, page_table, state_pool, state_conv,
              norm_pre, w_in, q_norm, w_uq, kv_norm, w_uk, w_uv, pool_w, pool_scale,
              conv_dw, conv_b, conv_ln_g, conv_ln_b, conv_pw, w_out, norm_post):
    Bp, Lp, _ = x_prompt.shape
    Bs, Ls, _ = x_sample.shape
    past = page_table.shape[1] * PAGE_SIZE
    pos_p = jnp.arange(Lp, dtype=jnp.int32)
    pos_s = past + jnp.arange(Ls, dtype=jnp.int32)
    hp, hs = x_prompt, x_sample
    ckv_p, kr_p, pool_p, conv_p = [], [], [], []
    ckv_s, kr_s, pool_s, conv_s = [], [], [], []
    for l in range(DEPTH):
        p = {'norm_pre': norm_pre[l], 'w_in': w_in[l], 'q_norm': q_norm[l], 'w_uq': w_uq[l],
             'kv_norm': kv_norm[l], 'w_uk': w_uk[l], 'w_uv': w_uv[l], 'pool_w': pool_w[l],
             'pool_scale': pool_scale[l], 'conv_dw': conv_dw[l], 'conv_b': conv_b[l],
             'conv_ln_g': conv_ln_g[l], 'conv_ln_b': conv_ln_b[l], 'conv_pw': conv_pw[l],
             'w_out': w_out[l], 'norm_post': norm_post[l]}
        pool0 = jnp.zeros((Bp, POOL_HIST, POOL_W), x_prompt.dtype)
        conv0 = jnp.zeros((Bp, CONV_HIST, CONV_W), x_prompt.dtype)
        attend_p = lambda qa, qr, ck, kr: attend_prompt(qa, qr, ck, kr, pos_p)
        hp, a1, a2, a3, a4 = layer(hp, pos_p, p, attend_p, pool0, conv0)
        ckv_p.append(a1); kr_p.append(a2); pool_p.append(a3); conv_p.append(a4)
        ckv_past = cache_ckv[l][page_table].reshape(Bs, past, KV_RANK)
        kr_past = cache_krope[l][page_table].reshape(Bs, past, ROPE_DIM)
        attend_s = lambda qa, qr, ck, kr: attend_sample(qa, qr, ckv_past, kr_past, ck, kr)
        hs, b1, b2, b3, b4 = layer(hs, pos_s, p, attend_s, state_pool[l], state_conv[l])
        ckv_s.append(b1); kr_s.append(b2); pool_s.append(b3); conv_s.append(b4)
    return (hp, hs,
            jnp.stack(ckv_p), jnp.stack(kr_p), jnp.stack(pool_p), jnp.stack(conv_p),
            jnp.stack(ckv_s), jnp.stack(kr_s), jnp.stack(pool_s), jnp.stack(conv_s))
```

```python
import functools

import jax
import jax.numpy as jnp
from jax import lax
from jax.experimental import pallas as pl
from jax.experimental.pallas import tpu as pltpu

D_MODEL = 1024
N_HEADS = 8
NOPE_DIM = 64
ROPE_DIM = 32
V_DIM = 64
Q_RANK = 256
KV_RANK = 128
ATT_W = N_HEADS * V_DIM
POOL_W = 256
POOL_GW = 64
POOL_HIST = 15
CONV_W = 256
CONV_K = 31
CONV_HIST = CONV_K - 1
PAGE_SIZE = 128
ATT_SCALE = (NOPE_DIM + ROPE_DIM) ** -0.5
ROPE_THETA = 10000.0
EPS = 1e-6

LANES = 128
KCAT_W = 2 * LANES
ONES_COL = KV_RANK + ROPE_DIM
NEG = -1e30

C_Q, C_KV, C_POOL, C_CONV, C_GATE, C_KR, C_END = 0, 256, 384, 640, 1152, 2176, 2304
U_ROPE, U_END = N_HEADS * NOPE_DIM, N_HEADS * NOPE_DIM + N_HEADS * LANES

VMEM_LIMIT = 56 * 1024 * 1024

f32 = jnp.float32
bf16 = jnp.bfloat16


def _dot(a, b):
    return jnp.dot(a, b, preferred_element_type=f32)


def _dot_nt(a, b):
    return lax.dot_general(a, b, (((1,), (1,)), ((), ())), preferred_element_type=f32)


def _rms(x, g):
    return x * lax.rsqrt(jnp.mean(x * x, axis=-1, keepdims=True) + EPS) * g


def _sigmoid(x):
    return 1.0 / (1.0 + jnp.exp(-x))


def _align8(n):
    return (n + 7) // 8 * 8


def _pre_kernel(x_ref, t1_ref, hp_ref, hc_ref, npre_ref, win_ref, qn_ref, wuq_ref, kvn_ref, wuk_ref,
                poolw_ref, pscale_ref, dw_ref, db_ref, lng_ref, lnb_ref, pw_ref,
                q_ref, kcat_ref, ckv_ref, kr_ref, mix_ref, gatt_ref, ptail_ref, ctail_ref,
                pext, cext, buf2, buf4, buf8, ybuf, *, tm, stride, ph, ch, pos0, tail, carry):
    i = pl.program_id(0)

    @pl.when(i == 0)
    def _():
        pext[0:ph, :] = hp_ref[...]
        cext[0:ch, :] = hc_ref[...]

    h = _rms(x_ref[...], npre_ref[...]).astype(bf16)
    t1 = t1_ref[...]
    lane = lax.broadcasted_iota(jnp.int32, (tm, LANES), 1)

    def rope(slot):
        pr = slot * t1
        return pr + pltpu.roll(pr, LANES - ROPE_DIM, 1)

    cq = _rms(_dot(h, win_ref[:, C_Q:C_KV]), qn_ref[...]).astype(bf16)
    qz = _dot(cq, wuq_ref[...])
    qabs = _dot(qz[:, 0:U_ROPE].astype(bf16), wuk_ref[...])
    for hh in range(N_HEADS):
        q_ref[hh, :, 0:LANES] = (qabs[:, hh * LANES:(hh + 1) * LANES] * ATT_SCALE).astype(bf16)
        qr = rope(qz[:, U_ROPE + hh * LANES:U_ROPE + (hh + 1) * LANES])
        q_ref[hh, :, LANES:KCAT_W] = jnp.where(lane < ROPE_DIM, qr * ATT_SCALE, 0.0).astype(bf16)

    ckv = _rms(_dot(h, win_ref[:, C_KV:C_POOL]), kvn_ref[...])
    ckv_ref[...] = ckv
    kr = rope(_dot(h, win_ref[:, C_KR:C_END]))
    kr_ref[...] = kr[:, 0:ROPE_DIM]
    kcat_ref[:, 0:LANES] = ckv.astype(bf16)
    kcat_ref[:, LANES:KCAT_W] = jnp.where(
        lane < ROPE_DIM, kr, jnp.where(lane == ROPE_DIM, 1.0, 0.0)).astype(bf16)

    sgate = _dot(h, win_ref[:, C_GATE:C_KR])
    sgate = sgate * _sigmoid(sgate)
    gatt_ref[...] = sgate[:, 0:ATT_W]

    n = ph + tm
    pext[ph:n, :] = _dot(h, win_ref[:, C_POOL:C_CONV])
    lo2 = _align8(stride)
    lo4 = _align8(lo2 + 2 * stride)
    lo8 = _align8(lo4 + 4 * stride)
    buf2[lo2:n, :] = pext[lo2:n, :] + pext[lo2 - stride:n - stride, :]
    buf4[lo4:n, :] = buf2[lo4:n, :] + buf2[lo4 - 2 * stride:n - 2 * stride, :]
    buf8[lo8:n, :] = buf4[lo8:n, :] + buf4[lo8 - 4 * stride:n - 4 * stride, :]
    s16 = buf8[ph:n, :] + buf8[ph - 8 * stride:n - 8 * stride, :]
    col = lax.broadcasted_iota(jnp.int32, (tm, POOL_W), 1)
    row = lax.broadcasted_iota(jnp.int32, (tm, POOL_W), 0)
    wsum = jnp.where(col < POOL_GW, buf2[ph:n, :],
                     jnp.where(col < 2 * POOL_GW, buf4[ph:n, :],
                               jnp.where(col < 3 * POOL_GW, buf8[ph:n, :], s16)))
    win = jnp.where(col < POOL_GW, 2, jnp.where(col < 2 * POOL_GW, 4, jnp.where(col < 3 * POOL_GW, 8, 16)))
    pos = pos0 + (i * tm + row) // stride
    cnt = jnp.minimum(pos + 1, win).astype(f32)
    d = wsum / cnt - pext[ph:n, :]
    ypool = _dot(d.astype(bf16), poolw_ref[...]) * pscale_ref[...]
    mix_ref[:, 0:POOL_W] = (ypool * sgate[:, ATT_W:ATT_W + POOL_W]).astype(bf16)

    uc = _dot(h, win_ref[:, C_CONV:C_GATE])
    cext[ch:ch + tm, :] = uc[:, 0:CONV_W] * _sigmoid(uc[:, CONV_W:2 * CONV_W])
    rc = min(tm, 128)
    for r0 in range(0, tm, rc):
        acc = jnp.broadcast_to(db_ref[...], (rc, CONV_W))
        for k in range(CONV_K):
            off = ch + r0 - (CONV_HIST - k) * stride
            acc = acc + cext[off:off + rc, :] * dw_ref[k:k + 1, :]
        ybuf[r0:r0 + rc, :] = acc
    y = ybuf[...]
    mu = jnp.mean(y, axis=-1, keepdims=True)
    yc = y - mu
    var = jnp.mean(yc * yc, axis=-1, keepdims=True)
    yn = yc * lax.rsqrt(var + EPS) * lng_ref[...] + lnb_ref[...]
    yconv = _dot((yn * _sigmoid(yn)).astype(bf16), pw_ref[...])
    mix_ref[:, POOL_W:POOL_W + CONV_W] = (yconv * sgate[:, ATT_W + POOL_W:]).astype(bf16)

    ptail_ref[...] = pext[n - tail:n, :]
    ctail_ref[...] = cext[ch + tm - tail:ch + tm, :]
    if carry:
        pext[0:ph, :] = pext[tm:tm + ph, :]
        cext[0:ch, :] = cext[tm:tm + ch, :]


def _pre_call(x, t1, hist_pool, hist_conv, w, *, tm, stride, pos0):
    rows = x.shape[0]
    ph, ch = hist_pool.shape[0], hist_conv.shape[0]
    nt = rows // tm
    tail = min(tm, _align8(CONV_HIST * stride))
    kern = functools.partial(_pre_kernel, tm=tm, stride=stride, ph=ph, ch=ch, pos0=pos0, tail=tail,
                             carry=nt > 1)
    rowblk = lambda width: pl.BlockSpec((tm, width), lambda i: (i, 0))
    full = lambda a: pl.BlockSpec(a.shape, lambda i: (0,) * a.ndim)
    weights = (w['norm_pre'], w['w_in'], w['q_norm'], w['w_uq'], w['kv_norm'], w['w_uk'], w['pool_w'],
               w['pool_scale'], w['conv_dw'], w['conv_b'], w['conv_ln_g'], w['conv_ln_b'], w['conv_pw'])
    return pl.pallas_call(
        kern,
        grid=(nt,),
        in_specs=[rowblk(D_MODEL), rowblk(LANES), full(hist_pool), full(hist_conv)] + [full(a) for a in weights],
        out_specs=[
            pl.BlockSpec((N_HEADS, tm, KCAT_W), lambda i: (0, i, 0)),
            rowblk(KCAT_W), rowblk(KV_RANK), rowblk(ROPE_DIM), rowblk(POOL_W + CONV_W), rowblk(ATT_W),
            pl.BlockSpec((tail, POOL_W), lambda i: (0, 0)),
            pl.BlockSpec((tail, CONV_W), lambda i: (0, 0)),
        ],
        out_shape=[
            jax.ShapeDtypeStruct((N_HEADS, rows, KCAT_W), bf16),
            jax.ShapeDtypeStruct((rows, KCAT_W), bf16),
            jax.ShapeDtypeStruct((rows, KV_RANK), f32),
            jax.ShapeDtypeStruct((rows, ROPE_DIM), f32),
            jax.ShapeDtypeStruct((rows, POOL_W + CONV_W), bf16),
            jax.ShapeDtypeStruct((rows, ATT_W), f32),
            jax.ShapeDtypeStruct((tail, POOL_W), f32),
            jax.ShapeDtypeStruct((tail, CONV_W), f32),
        ],
        scratch_shapes=[
            pltpu.VMEM((ph + tm, POOL_W), f32), pltpu.VMEM((ch + tm, CONV_W), f32),
            pltpu.VMEM((ph + tm, POOL_W), f32), pltpu.VMEM((ph + tm, POOL_W), f32),
            pltpu.VMEM((ph + tm, POOL_W), f32), pltpu.VMEM((tm, CONV_W), f32),
        ],
        compiler_params=pltpu.CompilerParams(dimension_semantics=("arbitrary",), vmem_limit_bytes=VMEM_LIMIT),
        name="pre",
    )(x, t1, hist_pool, hist_conv, *weights)


def _attn_prompt_kernel(q_ref, k_ref, o_ref, m_scr, acc_scr, *, tq, tk):
    i = pl.program_id(0)
    m_rows = N_HEADS * tq
    q = q_ref[...].reshape(m_rows, KCAT_W)
    m_scr[...] = jnp.full((m_rows, LANES), NEG, f32)
    acc_scr[...] = jnp.zeros((m_rows, KCAT_W), f32)

    def tile(j, masked):
        k = k_ref[pl.ds(pl.multiple_of(j * tk, tk), tk), :]
        s = _dot_nt(q, k)
        if masked:
            qpos = i * tq + (lax.broadcasted_iota(jnp.int32, (m_rows, tk), 0) & (tq - 1))
            kpos = j * tk + lax.broadcasted_iota(jnp.int32, (m_rows, tk), 1)
            s = jnp.where(kpos <= qpos, s, NEG)
        m_prev = m_scr[...]
        m_new = jnp.maximum(m_prev, jnp.max(s, axis=1, keepdims=True))
        alpha = jnp.exp(m_prev - m_new)
        p = jnp.exp(s - jnp.concatenate([m_new] * (tk // LANES), axis=1))
        pv = _dot(p.astype(bf16), k)
        acc_scr[...] = acc_scr[...] * jnp.concatenate([alpha] * (KCAT_W // LANES), axis=1) + pv
        m_scr[...] = m_new

    nfull = (i * tq) // tk

    def body(j, c):
        tile(j, False)
        return c

    lax.fori_loop(0, nfull, body, 0)
    tile(nfull, True)

    acc = acc_scr[...]
    out = acc[:, 0:KV_RANK] / acc[:, ONES_COL:ONES_COL + 1]
    o_ref[...] = out.astype(bf16).reshape(N_HEADS, tq, KV_RANK)


def _attn_prompt_call(q, kcat, *, tq, tk):
    rows = kcat.shape[0]
    kern = functools.partial(_attn_prompt_kernel, tq=tq, tk=tk)
    return pl.pallas_call(
        kern,
        grid=(rows // tq,),
        in_specs=[pl.BlockSpec((N_HEADS, tq, KCAT_W), lambda i: (0, i, 0)),
                  pl.BlockSpec((rows, KCAT_W), lambda i: (0, 0))],
        out_specs=pl.BlockSpec((N_HEADS, tq, KV_RANK), lambda i: (0, i, 0)),
        out_shape=jax.ShapeDtypeStruct((N_HEADS, rows, KV_RANK), bf16),
        scratch_shapes=[pltpu.VMEM((N_HEADS * tq, LANES), f32), pltpu.VMEM((N_HEADS * tq, KCAT_W), f32)],
        compiler_params=pltpu.CompilerParams(dimension_semantics=("arbitrary",), vmem_limit_bytes=VMEM_LIMIT),
        name="attn_prompt",
    )(q, kcat)


def _attn_sample_kernel(pt_ref, q_ref, knew_ref, ckv_hbm, kr_hbm, o_ref,
                        kbuf, rbuf, sem, m_scr, l_scr, acc_scr, *, layer, n_batch, n_chunks, cp, n_new):
    b = pl.program_id(0)
    c = pl.program_id(1)
    step = b * n_chunks + c
    slot = step % 2
    rows_q = q_ref.shape[1]

    def page_copies(bb, cc, sl, p):
        page = pt_ref[bb, cc * cp + p]
        dst = pl.ds(p * PAGE_SIZE, PAGE_SIZE)
        return (pltpu.make_async_copy(ckv_hbm.at[layer, page], kbuf.at[sl, dst], sem.at[0, sl]),
                pltpu.make_async_copy(kr_hbm.at[layer, page], rbuf.at[sl, dst], sem.at[1, sl]))

    def issue(bb, cc, sl):
        for p in range(cp):
            for cpy in page_copies(bb, cc, sl, p):
                cpy.start()

    @pl.when(step == 0)
    def _():
        issue(0, 0, 0)

    @pl.when(step + 1 < n_batch * n_chunks)
    def _():
        nxt = step + 1
        issue(nxt // n_chunks, nxt % n_chunks, 1 - slot)

    for p in range(cp):
        for cpy in page_copies(b, c, slot, p):
            cpy.wait()

    @pl.when(c == 0)
    def _():
        m_scr[...] = jnp.full(m_scr.shape, NEG, f32)
        l_scr[...] = jnp.zeros(l_scr.shape, f32)
        acc_scr[...] = jnp.zeros(acc_scr.shape, f32)

    q = q_ref[0]

    def update(s, v):
        m_prev = m_scr[...]
        m_new = jnp.maximum(m_prev, jnp.max(s, axis=1, keepdims=True))
        alpha = jnp.exp(m_prev - m_new)
        p = jnp.exp(s - m_new[:, 0:1])
        l_scr[...] = l_scr[...] * alpha + jnp.sum(p, axis=1, keepdims=True)
        acc_scr[...] = acc_scr[...] * alpha + _dot(p.astype(bf16), v)
        m_scr[...] = m_new

    kb = kbuf[slot].astype(bf16)
    rb = rbuf[slot].astype(bf16)
    s = _dot_nt(q[:, 0:KV_RANK], kb) + _dot_nt(q[:, KV_RANK:KV_RANK + ROPE_DIM], rb)
    update(s, kb)

    @pl.when(c == n_chunks - 1)
    def _():
        kn = knew_ref[0]
        sn = _dot_nt(q, kn)
        t_q = lax.broadcasted_iota(jnp.int32, sn.shape, 0) % n_new
        t_k = lax.broadcasted_iota(jnp.int32, sn.shape, 1)
        update(jnp.where(t_k <= t_q, sn, NEG), kn[:, 0:KV_RANK])
        o_ref[0] = (acc_scr[...] / l_scr[...]).astype(bf16)


def _attn_sample_call(page_table, q, knew, cache_ckv, cache_krope, *, layer, cp):
    n_batch, n_pages = page_table.shape
    n_chunks = n_pages // cp
    rows_q = q.shape[1]
    n_new = rows_q // N_HEADS
    kern = functools.partial(_attn_sample_kernel, layer=layer, n_batch=n_batch, n_chunks=n_chunks, cp=cp,
                             n_new=n_new)
    grid_spec = pltpu.PrefetchScalarGridSpec(
        num_scalar_prefetch=1,
        grid=(n_batch, n_chunks),
        in_specs=[pl.BlockSpec((1, rows_q, KCAT_W), lambda b, c, pt: (b, 0, 0)),
                  pl.BlockSpec((1,) + knew.shape[1:], lambda b, c, pt: (b, 0, 0)),
                  pl.BlockSpec(memory_space=pl.ANY),
                  pl.BlockSpec(memory_space=pl.ANY)],
        out_specs=pl.BlockSpec((1, rows_q, KV_RANK), lambda b, c, pt: (b, 0, 0)),
        scratch_shapes=[pltpu.VMEM((2, cp * PAGE_SIZE, KV_RANK), f32),
                        pltpu.VMEM((2, cp * PAGE_SIZE, ROPE_DIM), f32),
                        pltpu.SemaphoreType.DMA((2, 2)),
                        pltpu.VMEM((rows_q, LANES), f32), pltpu.VMEM((rows_q, LANES), f32),
                        pltpu.VMEM((rows_q, KV_RANK), f32)],
    )
    return pl.pallas_call(
        kern,
        grid_spec=grid_spec,
        out_shape=jax.ShapeDtypeStruct((n_batch, rows_q, KV_RANK), bf16),
        compiler_params=pltpu.CompilerParams(dimension_semantics=("arbitrary", "arbitrary"),
                                             vmem_limit_bytes=VMEM_LIMIT),
        name="attn_sample",
    )(page_table, q, knew, cache_ckv, cache_krope)


def _post_kernel(x_ref, o_ref, gatt_ref, mix_ref, wuv_ref, wout_ref, npost_ref, y_ref):
    olat = jnp.concatenate([o_ref[hh] for hh in range(N_HEADS)], axis=1)
    oatt = _dot(olat, wuv_ref[...])
    mixed = jnp.concatenate([(oatt * gatt_ref[...]).astype(bf16), mix_ref[...]], axis=1)
    out = _dot(mixed, wout_ref[...])
    y_ref[...] = x_ref[...] + _rms(out, npost_ref[...])


def _post_call(x, olat, gatt, mix, w, *, tm):
    rows = x.shape[0]
    rowblk = lambda width: pl.BlockSpec((tm, width), lambda i: (i, 0))
    full = lambda a: pl.BlockSpec(a.shape, lambda i: (0,) * a.ndim)
    return pl.pallas_call(
        _post_kernel,
        grid=(rows // tm,),
        in_specs=[rowblk(D_MODEL), pl.BlockSpec((N_HEADS, tm, KV_RANK), lambda i: (0, i, 0)),
                  rowblk(ATT_W), rowblk(POOL_W + CONV_W), full(w['w_uv']), full(w['w_out']), full(w['norm_post'])],
        out_specs=rowblk(D_MODEL),
        out_shape=jax.ShapeDtypeStruct((rows, D_MODEL), f32),
        compiler_params=pltpu.CompilerParams(dimension_semantics=("arbitrary",), vmem_limit_bytes=VMEM_LIMIT),
        name="post",
    )(x, olat, gatt, mix, w['w_uv'], w['w_out'], w['norm_post'])


def _swap_halves(wcols):
    half = wcols.shape[-1] // 2
    return jnp.concatenate([wcols[..., half:], wcols[..., :half]], axis=-1)


def _rope_slot(wcols):
    pad = jnp.zeros((wcols.shape[0], LANES - 2 * ROPE_DIM), wcols.dtype)
    return jnp.concatenate([wcols, _swap_halves(wcols), pad], axis=-1)


def _block_diag(blocks):
    g, a, b = blocks.shape
    eye = jnp.eye(g, dtype=blocks.dtype)
    return jnp.einsum('gab,gh->gahb', blocks, eye).reshape(g * a, g * b)


def _prep_layer(l, norm_pre, w_in, q_norm, w_uq, kv_norm, w_uk, w_uv, pool_w, pool_scale,
                conv_dw, conv_b, conv_ln_g, conv_ln_b, conv_pw, w_out, norm_post):
    wi = w_in[l]
    o_kr = Q_RANK + KV_RANK
    o_pool = o_kr + ROPE_DIM
    win = jnp.concatenate([wi[:, :o_kr], wi[:, o_pool:], _rope_slot(wi[:, o_kr:o_pool])], axis=1)
    wq = w_uq[l].reshape(Q_RANK, N_HEADS, NOPE_DIM + ROPE_DIM)
    wuq = jnp.concatenate(
        [wq[:, :, :NOPE_DIM].reshape(Q_RANK, N_HEADS * NOPE_DIM)]
        + [_rope_slot(wq[:, hh, NOPE_DIM:]) for hh in range(N_HEADS)], axis=1)
    row = lambda v: v.reshape(1, -1)
    return {
        'norm_pre': row(norm_pre[l]), 'w_in': win.astype(bf16), 'q_norm': row(q_norm[l]),
        'w_uq': wuq.astype(bf16), 'kv_norm': row(kv_norm[l]),
        'w_uk': _block_diag(jnp.transpose(w_uk[l], (1, 2, 0))).astype(bf16),
        'w_uv': _block_diag(jnp.transpose(w_uv[l], (1, 0, 2))).astype(bf16),
        'pool_w': _block_diag(pool_w[l]).astype(bf16), 'pool_scale': row(pool_scale[l]),
        'conv_dw': conv_dw[l], 'conv_b': row(conv_b[l]), 'conv_ln_g': row(conv_ln_g[l]),
        'conv_ln_b': row(conv_ln_b[l]), 'conv_pw': conv_pw[l].astype(bf16),
        'w_out': w_out[l].astype(bf16), 'norm_post': row(norm_post[l]),
    }


def _rope_table(pos):
    inv = ROPE_THETA ** (-jnp.arange(0, ROPE_DIM, 2, dtype=f32) / ROPE_DIM)
    ang = pos.astype(f32)[:, None] * inv[None, :]
    cos, sin = jnp.cos(ang), jnp.sin(ang)
    pad = jnp.zeros((pos.shape[0], LANES - 2 * ROPE_DIM), f32)
    return jnp.concatenate([cos, cos, -sin, sin, pad], axis=1)


def kernel(x_prompt, x_sample, cache_ckv, cache_krope, page_table, state_pool, state_conv, norm_pre, w_in,
           q_norm, w_uq, kv_norm, w_uk, w_uv, pool_w, pool_scale, conv_dw, conv_b, conv_ln_g, conv_ln_b,
           conv_pw, w_out, norm_post):
    bp, lp, _ = x_prompt.shape
    bs, ls, _ = x_sample.shape
    assert bp == 1
    depth = w_in.shape[0]
    past = page_table.shape[1] * PAGE_SIZE
    rows_s = bs * ls

    t1_p = _rope_table(jnp.arange(lp, dtype=jnp.int32))
    t1_s = _rope_table(past + jnp.repeat(jnp.arange(ls, dtype=jnp.int32), bs))
    hp = x_prompt.reshape(lp, D_MODEL)
    hs = jnp.transpose(x_sample, (1, 0, 2)).reshape(rows_s, D_MODEL)
    zero_hist = jnp.zeros((32, POOL_W), f32)

    def from_time_major(a, width):
        return jnp.transpose(a.reshape(ls, bs, width), (1, 0, 2))

    outs = {k: [] for k in ('ckv_p', 'kr_p', 'pool_p', 'conv_p', 'ckv_s', 'kr_s', 'pool_s', 'conv_s')}
    for l in range(depth):
        w = _prep_layer(l, norm_pre, w_in, q_norm, w_uq, kv_norm, w_uk, w_uv, pool_w, pool_scale,
                        conv_dw, conv_b, conv_ln_g, conv_ln_b, conv_pw, w_out, norm_post)
        q, kcat, ckv, kr, mix, gatt, ptail, ctail = _pre_call(
            hp, t1_p, zero_hist, zero_hist, w, tm=512, stride=1, pos0=0)
        olat = _attn_prompt_call(q, kcat, tq=128, tk=512)
        hp = _post_call(hp, olat, gatt, mix, w, tm=512)
        outs['ckv_p'].append(ckv.reshape(bp, lp, KV_RANK))
        outs['kr_p'].append(kr.reshape(bp, lp, ROPE_DIM))
        outs['pool_p'].append(ptail[-POOL_HIST:].reshape(bp, POOL_HIST, POOL_W))
        outs['conv_p'].append(ctail[-CONV_HIST:].reshape(bp, CONV_HIST, CONV_W))
        hist_pool = jnp.transpose(state_pool[l], (1, 0, 2)).reshape(POOL_HIST * bs, POOL_W)
        hist_conv = jnp.transpose(state_conv[l], (1, 0, 2)).reshape(CONV_HIST * bs, CONV_W)
        q, kcat, ckv, kr, mix, gatt, ptail, ctail = _pre_call(
            hs, t1_s, hist_pool, hist_conv, w, tm=rows_s, stride=bs, pos0=past)
        q_b = jnp.transpose(q.reshape(N_HEADS, ls, bs, KCAT_W), (2, 0, 1, 3)).reshape(bs, N_HEADS * ls, KCAT_W)
        knew = jnp.pad(from_time_major(kcat, KCAT_W), ((0, 0), (0, 16 - ls), (0, 0)))
        o_b = _attn_sample_call(page_table, q_b, knew, cache_ckv, cache_krope, layer=l, cp=32)
        olat = jnp.transpose(o_b.reshape(bs, N_HEADS, ls, KV_RANK), (1, 2, 0, 3)).reshape(N_HEADS, rows_s, KV_RANK)
        hs = _post_call(hs, olat, gatt, mix, w, tm=rows_s)
        outs['ckv_s'].append(from_time_major(ckv, KV_RANK))
        outs['kr_s'].append(from_time_major(kr, ROPE_DIM))
        outs['pool_s'].append(jnp.concatenate([state_pool[l][:, ls:], from_time_major(ptail, POOL_W)], axis=1))
        outs['conv_s'].append(jnp.concatenate([state_conv[l][:, ls:], from_time_major(ctail, CONV_W)], axis=1))

    y_prompt = hp.reshape(bp, lp, D_MODEL)
    y_sample = from_time_major(hs, D_MODEL)
    return (y_prompt, y_sample,
            jnp.stack(outs['ckv_p']), jnp.stack(outs['kr_p']), jnp.stack(outs['pool_p']), jnp.stack(outs['conv_p']),
            jnp.stack(outs['ckv_s']), jnp.stack(outs['kr_s']), jnp.stack(outs['pool_s']), jnp.stack(outs['conv_s']))
```

```python
import functools
import math

import jax
import jax.numpy as jnp
from jax import lax
from jax.experimental import pallas as pl
from jax.experimental.pallas import tpu as pltpu

D_MODEL = 1024
N_HEADS = 8
NOPE_DIM = 64
ROPE_DIM = 32
V_DIM = 64
Q_RANK = 256
KV_RANK = 128
ATT_W = N_HEADS * V_DIM
POOL_W = 256
POOL_GW = 64
POOL_HIST = 15
CONV_W = 256
CONV_K = 31
CONV_HIST = CONV_K - 1
PAGE_SIZE = 128
ATT_SCALE = (NOPE_DIM + ROPE_DIM) ** -0.5
Q_SCALE = ATT_SCALE * math.log2(math.e)
ROPE_THETA = 10000.0
EPS = 1e-6

LANES = 128
KCAT_W = 2 * LANES
ONES_COL = KV_RANK + ROPE_DIM
NEG = -1e30

C_Q, C_KV, C_POOL, C_CONV, C_GATE, C_KR, C_END = 0, 256, 384, 640, 1152, 2176, 2304
U_ROPE, U_END = N_HEADS * NOPE_DIM, N_HEADS * NOPE_DIM + N_HEADS * LANES

VMEM_LIMIT = 56 * 1024 * 1024

f32 = jnp.float32
bf16 = jnp.bfloat16


def _dot(a, b):
    return jnp.dot(a, b, preferred_element_type=f32)


def _dot_nt(a, b):
    return lax.dot_general(a, b, (((1,), (1,)), ((), ())), preferred_element_type=f32)


def _rms(x, g):
    return x * lax.rsqrt(jnp.mean(x * x, axis=-1, keepdims=True) + EPS) * g


def _sigmoid(x):
    return 1.0 / (1.0 + jnp.exp(-x))


def _align8(n):
    return (n + 7) // 8 * 8


def _pre_kernel(x_ref, t1_ref, hp_ref, hc_ref, npre_ref, win_ref, qn_ref, wuq_ref, kvn_ref, wuk_ref,
                poolw_ref, pscale_ref, dw_ref, db_ref, lng_ref, lnb_ref, pw_ref,
                q_ref, kcat_ref, ckv_ref, kr_ref, mix_ref, gatt_ref, ptail_ref, ctail_ref,
                pext, cext, buf2, buf4, buf8, ybuf, *, tm, stride, ph, ch, pos0, tail, carry):
    i = pl.program_id(0)

    @pl.when(i == 0)
    def _():
        pext[0:ph, :] = hp_ref[...]
        cext[0:ch, :] = hc_ref[...]

    h = _rms(x_ref[...], npre_ref[...]).astype(bf16)
    t1 = t1_ref[...]
    lane = lax.broadcasted_iota(jnp.int32, (tm, LANES), 1)

    def rope(slot):
        pr = slot * t1
        return pr + pltpu.roll(pr, LANES - ROPE_DIM, 1)

    cq = _rms(_dot(h, win_ref[:, C_Q:C_KV]), qn_ref[...]).astype(bf16)
    qz = _dot(cq, wuq_ref[...])
    qabs = _dot(qz[:, 0:U_ROPE].astype(bf16), wuk_ref[...])
    for hh in range(N_HEADS):
        q_ref[hh, :, 0:LANES] = (qabs[:, hh * LANES:(hh + 1) * LANES] * Q_SCALE).astype(bf16)
        qr = rope(qz[:, U_ROPE + hh * LANES:U_ROPE + (hh + 1) * LANES])
        q_ref[hh, :, LANES:KCAT_W] = jnp.where(lane < ROPE_DIM, qr * Q_SCALE, 0.0).astype(bf16)

    ckv = _rms(_dot(h, win_ref[:, C_KV:C_POOL]), kvn_ref[...])
    ckv_ref[...] = ckv
    kr = rope(_dot(h, win_ref[:, C_KR:C_END]))
    kr_ref[...] = kr[:, 0:ROPE_DIM]
    kcat_ref[:, 0:LANES] = ckv.astype(bf16)
    kcat_ref[:, LANES:KCAT_W] = jnp.where(
        lane < ROPE_DIM, kr, jnp.where(lane == ROPE_DIM, 1.0, 0.0)).astype(bf16)

    sgate = _dot(h, win_ref[:, C_GATE:C_KR])
    sgate = sgate * _sigmoid(sgate)
    gatt_ref[...] = sgate[:, 0:ATT_W]

    n = ph + tm
    pext[ph:n, :] = _dot(h, win_ref[:, C_POOL:C_CONV])
    lo2 = _align8(stride)
    lo4 = _align8(lo2 + 2 * stride)
    lo8 = _align8(lo4 + 4 * stride)
    buf2[lo2:n, :] = pext[lo2:n, :] + pext[lo2 - stride:n - stride, :]
    buf4[lo4:n, :] = buf2[lo4:n, :] + buf2[lo4 - 2 * stride:n - 2 * stride, :]
    buf8[lo8:n, :] = buf4[lo8:n, :] + buf4[lo8 - 4 * stride:n - 4 * stride, :]
    s16 = buf8[ph:n, :] + buf8[ph - 8 * stride:n - 8 * stride, :]
    col = lax.broadcasted_iota(jnp.int32, (tm, POOL_W), 1)
    row = lax.broadcasted_iota(jnp.int32, (tm, POOL_W), 0)
    wsum = jnp.where(col < POOL_GW, buf2[ph:n, :],
                     jnp.where(col < 2 * POOL_GW, buf4[ph:n, :],
                               jnp.where(col < 3 * POOL_GW, buf8[ph:n, :], s16)))
    win = jnp.where(col < POOL_GW, 2, jnp.where(col < 2 * POOL_GW, 4, jnp.where(col < 3 * POOL_GW, 8, 16)))
    pos = pos0 + (i * tm + row) // stride
    cnt = jnp.minimum(pos + 1, win).astype(f32)
    d = wsum / cnt - pext[ph:n, :]
    ypool = _dot(d.astype(bf16), poolw_ref[...]) * pscale_ref[...]
    mix_ref[:, 0:POOL_W] = (ypool * sgate[:, ATT_W:ATT_W + POOL_W]).astype(bf16)

    uc = _dot(h, win_ref[:, C_CONV:C_GATE])
    cext[ch:ch + tm, :] = uc[:, 0:CONV_W] * _sigmoid(uc[:, CONV_W:2 * CONV_W])
    rc = min(tm, 128)
    for r0 in range(0, tm, rc):
        acc = jnp.broadcast_to(db_ref[...], (rc, CONV_W))
        for k in range(CONV_K):
            off = ch + r0 - (CONV_HIST - k) * stride
            acc = acc + cext[off:off + rc, :] * dw_ref[k:k + 1, :]
        ybuf[r0:r0 + rc, :] = acc
    y = ybuf[...]
    mu = jnp.mean(y, axis=-1, keepdims=True)
    yc = y - mu
    var = jnp.mean(yc * yc, axis=-1, keepdims=True)
    yn = yc * lax.rsqrt(var + EPS) * lng_ref[...] + lnb_ref[...]
    yconv = _dot((yn * _sigmoid(yn)).astype(bf16), pw_ref[...])
    mix_ref[:, POOL_W:POOL_W + CONV_W] = (yconv * sgate[:, ATT_W + POOL_W:]).astype(bf16)

    ptail_ref[...] = pext[n - tail:n, :]
    ctail_ref[...] = cext[ch + tm - tail:ch + tm, :]
    if carry:
        pext[0:ph, :] = pext[tm:tm + ph, :]
        cext[0:ch, :] = cext[tm:tm + ch, :]


def _pre_call(x, t1, hist_pool, hist_conv, w, *, tm, stride, pos0):
    rows = x.shape[0]
    ph, ch = hist_pool.shape[0], hist_conv.shape[0]
    nt = rows // tm
    tail = min(tm, _align8(CONV_HIST * stride))
    kern = functools.partial(_pre_kernel, tm=tm, stride=stride, ph=ph, ch=ch, pos0=pos0, tail=tail,
                             carry=nt > 1)
    rowblk = lambda width: pl.BlockSpec((tm, width), lambda i: (i, 0))
    full = lambda a: pl.BlockSpec(a.shape, lambda i: (0,) * a.ndim)
    weights = (w['norm_pre'], w['w_in'], w['q_norm'], w['w_uq'], w['kv_norm'], w['w_uk'], w['pool_w'],
               w['pool_scale'], w['conv_dw'], w['conv_b'], w['conv_ln_g'], w['conv_ln_b'], w['conv_pw'])
    return pl.pallas_call(
        kern,
        grid=(nt,),
        in_specs=[rowblk(D_MODEL), rowblk(LANES), full(hist_pool), full(hist_conv)] + [full(a) for a in weights],
        out_specs=[
            pl.BlockSpec((N_HEADS, tm, KCAT_W), lambda i: (0, i, 0)),
            rowblk(KCAT_W), rowblk(KV_RANK), rowblk(ROPE_DIM), rowblk(POOL_W + CONV_W), rowblk(ATT_W),
            pl.BlockSpec((tail, POOL_W), lambda i: (0, 0)),
            pl.BlockSpec((tail, CONV_W), lambda i: (0, 0)),
        ],
        out_shape=[
            jax.ShapeDtypeStruct((N_HEADS, rows, KCAT_W), bf16),
            jax.ShapeDtypeStruct((rows, KCAT_W), bf16),
            jax.ShapeDtypeStruct((rows, KV_RANK), f32),
            jax.ShapeDtypeStruct((rows, ROPE_DIM), f32),
            jax.ShapeDtypeStruct((rows, POOL_W + CONV_W), bf16),
            jax.ShapeDtypeStruct((rows, ATT_W), f32),
            jax.ShapeDtypeStruct((tail, POOL_W), f32),
            jax.ShapeDtypeStruct((tail, CONV_W), f32),
        ],
        scratch_shapes=[
            pltpu.VMEM((ph + tm, POOL_W), f32), pltpu.VMEM((ch + tm, CONV_W), f32),
            pltpu.VMEM((ph + tm, POOL_W), f32), pltpu.VMEM((ph + tm, POOL_W), f32),
            pltpu.VMEM((ph + tm, POOL_W), f32), pltpu.VMEM((tm, CONV_W), f32),
        ],
        compiler_params=pltpu.CompilerParams(dimension_semantics=("arbitrary",), vmem_limit_bytes=VMEM_LIMIT),
        name="pre",
    )(x, t1, hist_pool, hist_conv, *weights)


def _attn_prompt_kernel(q_ref, k_ref, o_ref, m_scr, acc_scr, s_a, s_b, *, tq, tk):
    i = pl.program_id(0)
    m_rows = N_HEADS * tq
    q = q_ref[...].reshape(m_rows, KCAT_W)
    m_scr[...] = jnp.full((m_rows, LANES), NEG, f32)
    acc_scr[...] = jnp.zeros((m_rows, KCAT_W), f32)

    def key_tile(t):
        return k_ref[pl.ds(pl.multiple_of(t * tk, tk), tk), :]

    def scores(t, dst):
        dst[...] = _dot_nt(q, key_tile(t))

    def consume(src, t, masked):
        s = src[...]
        if masked:
            qpos = i * tq + (lax.broadcasted_iota(jnp.int32, (m_rows, tk), 0) & (tq - 1))
            kpos = t * tk + lax.broadcasted_iota(jnp.int32, (m_rows, tk), 1)
            s = jnp.where(kpos <= qpos, s, NEG)
        m_prev = m_scr[...]
        m_new = jnp.maximum(m_prev, jnp.max(s, axis=1, keepdims=True))
        alpha = jnp.exp2(m_prev - m_new)
        p = jnp.exp2(s - jnp.concatenate([m_new] * (tk // LANES), axis=1))
        pv = _dot(p.astype(bf16), key_tile(t))
        acc_scr[...] = acc_scr[...] * jnp.concatenate([alpha] * (KCAT_W // LANES), axis=1) + pv
        m_scr[...] = m_new

    nfull = (i * tq) // tk
    scores(0, s_a)

    def pair(jj, c):
        t = 2 * jj
        scores(t + 1, s_b)
        consume(s_a, t, False)
        scores(t + 2, s_a)
        consume(s_b, t + 1, False)
        return c

    lax.fori_loop(0, nfull // 2, pair, 0)

    @pl.when(nfull % 2 == 1)
    def _():
        scores(nfull, s_b)
        consume(s_a, nfull - 1, False)
        consume(s_b, nfull, True)

    @pl.when(nfull % 2 == 0)
    def _():
        consume(s_a, nfull, True)

    acc = acc_scr[...]
    out = acc[:, 0:KV_RANK] / acc[:, ONES_COL:ONES_COL + 1]
    o_ref[...] = out.astype(bf16).reshape(N_HEADS, tq, KV_RANK)


def _attn_prompt_call(q, kcat, *, tq, tk):
    rows = kcat.shape[0]
    kern = functools.partial(_attn_prompt_kernel, tq=tq, tk=tk)
    return pl.pallas_call(
        kern,
        grid=(rows // tq,),
        in_specs=[pl.BlockSpec((N_HEADS, tq, KCAT_W), lambda i: (0, i, 0)),
                  pl.BlockSpec((rows, KCAT_W), lambda i: (0, 0))],
        out_specs=pl.BlockSpec((N_HEADS, tq, KV_RANK), lambda i: (0, i, 0)),
        out_shape=jax.ShapeDtypeStruct((N_HEADS, rows, KV_RANK), bf16),
        scratch_shapes=[pltpu.VMEM((N_HEADS * tq, LANES), f32), pltpu.VMEM((N_HEADS * tq, KCAT_W), f32),
                        pltpu.VMEM((N_HEADS * tq, tk), f32), pltpu.VMEM((N_HEADS * tq, tk), f32)],
        compiler_params=pltpu.CompilerParams(dimension_semantics=("arbitrary",), vmem_limit_bytes=VMEM_LIMIT),
        name="attn_prompt",
    )(q, kcat)


def _attn_sample_kernel(pt_ref, q_ref, knew_ref, ckv_hbm, krt_hbm, o_ref, kbuf, rbuf, sem,
                        *, layer, n_batch, n_pages, n_new):
    b = pl.program_id(0)
    slot = b % 2

    def page_copies(bb, sl, p):
        page = pt_ref[bb, p]
        dst = pl.ds(p * PAGE_SIZE, PAGE_SIZE)
        return (pltpu.make_async_copy(ckv_hbm.at[layer, page], kbuf.at[sl, dst], sem.at[0, sl]),
                pltpu.make_async_copy(krt_hbm.at[layer, page], rbuf.at[sl, :, dst], sem.at[1, sl]))

    def issue(bb, sl):
        for p in range(n_pages):
            for cpy in page_copies(bb, sl, p):
                cpy.start()

    @pl.when(b == 0)
    def _():
        issue(0, 0)

    @pl.when(b + 1 < n_batch)
    def _():
        issue(b + 1, 1 - slot)

    for p in range(n_pages):
        for cpy in page_copies(b, slot, p):
            cpy.wait()

    q = q_ref[0]
    kn = knew_ref[0]
    kb = kbuf[slot].astype(bf16)
    rb = rbuf[slot].astype(bf16)
    s = _dot_nt(q[:, 0:KV_RANK], kb) + _dot(q[:, KV_RANK:KV_RANK + ROPE_DIM], rb)
    sn = _dot_nt(q, kn)
    t_q = lax.broadcasted_iota(jnp.int32, sn.shape, 0) % n_new
    t_k = lax.broadcasted_iota(jnp.int32, sn.shape, 1)
    sn = jnp.where(t_k <= t_q, sn, NEG)
    m = jnp.maximum(jnp.max(s, axis=1, keepdims=True), jnp.max(sn, axis=1, keepdims=True))
    p = jnp.exp2(s - m)
    pn = jnp.exp2(sn - m)
    l = jnp.sum(p, axis=1, keepdims=True) + jnp.sum(pn, axis=1, keepdims=True)
    o = _dot(p.astype(bf16), kb) + _dot(pn.astype(bf16), kn[:, 0:KV_RANK])
    o_ref[0] = (o / l).astype(bf16)


def _attn_sample_call(page_table, q, knew, cache_ckv, cache_krope_t, *, layer):
    n_batch, n_pages = page_table.shape
    past = n_pages * PAGE_SIZE
    rows_q = q.shape[1]
    kern = functools.partial(_attn_sample_kernel, layer=layer, n_batch=n_batch, n_pages=n_pages,
                             n_new=rows_q // N_HEADS)
    grid_spec = pltpu.PrefetchScalarGridSpec(
        num_scalar_prefetch=1,
        grid=(n_batch,),
        in_specs=[pl.BlockSpec((1, rows_q, KCAT_W), lambda b, pt: (b, 0, 0)),
                  pl.BlockSpec((1,) + knew.shape[1:], lambda b, pt: (b, 0, 0)),
                  pl.BlockSpec(memory_space=pl.ANY),
                  pl.BlockSpec(memory_space=pl.ANY)],
        out_specs=pl.BlockSpec((1, rows_q, KV_RANK), lambda b, pt: (b, 0, 0)),
        scratch_shapes=[pltpu.VMEM((2, past, KV_RANK), f32),
                        pltpu.VMEM((2, ROPE_DIM, past), f32),
                        pltpu.SemaphoreType.DMA((2, 2))],
    )
    return pl.pallas_call(
        kern,
        grid_spec=grid_spec,
        out_shape=jax.ShapeDtypeStruct((n_batch, rows_q, KV_RANK), bf16),
        compiler_params=pltpu.CompilerParams(dimension_semantics=("arbitrary",), vmem_limit_bytes=VMEM_LIMIT),
        name="attn_sample",
    )(page_table, q, knew, cache_ckv, cache_krope_t)


def _post_kernel(x_ref, o_ref, gatt_ref, mix_ref, wuv_ref, wout_ref, npost_ref, y_ref):
    olat = jnp.concatenate([o_ref[hh] for hh in range(N_HEADS)], axis=1)
    oatt = _dot(olat, wuv_ref[...])
    mixed = jnp.concatenate([(oatt * gatt_ref[...]).astype(bf16), mix_ref[...]], axis=1)
    out = _dot(mixed, wout_ref[...])
    y_ref[...] = x_ref[...] + _rms(out, npost_ref[...])


def _post_call(x, olat, gatt, mix, w, *, tm):
    rows = x.shape[0]
    rowblk = lambda width: pl.BlockSpec((tm, width), lambda i: (i, 0))
    full = lambda a: pl.BlockSpec(a.shape, lambda i: (0,) * a.ndim)
    return pl.pallas_call(
        _post_kernel,
        grid=(rows // tm,),
        in_specs=[rowblk(D_MODEL), pl.BlockSpec((N_HEADS, tm, KV_RANK), lambda i: (0, i, 0)),
                  rowblk(ATT_W), rowblk(POOL_W + CONV_W), full(w['w_uv']), full(w['w_out']), full(w['norm_post'])],
        out_specs=rowblk(D_MODEL),
        out_shape=jax.ShapeDtypeStruct((rows, D_MODEL), f32),
        compiler_params=pltpu.CompilerParams(dimension_semantics=("arbitrary",), vmem_limit_bytes=VMEM_LIMIT),
        name="post",
    )(x, olat, gatt, mix, w['w_uv'], w['w_out'], w['norm_post'])


def _swap_halves(wcols):
    half = wcols.shape[-1] // 2
    return jnp.concatenate([wcols[..., half:], wcols[..., :half]], axis=-1)


def _rope_slot(wcols):
    pad = jnp.zeros((wcols.shape[0], LANES - 2 * ROPE_DIM), wcols.dtype)
    return jnp.concatenate([wcols, _swap_halves(wcols), pad], axis=-1)


def _block_diag(blocks):
    g, a, b = blocks.shape
    eye = jnp.eye(g, dtype=blocks.dtype)
    return jnp.einsum('gab,gh->gahb', blocks, eye).reshape(g * a, g * b)


def _prep_layer(l, norm_pre, w_in, q_norm, w_uq, kv_norm, w_uk, w_uv, pool_w, pool_scale,
                conv_dw, conv_b, conv_ln_g, conv_ln_b, conv_pw, w_out, norm_post):
    wi = w_in[l]
    o_kr = Q_RANK + KV_RANK
    o_pool = o_kr + ROPE_DIM
    win = jnp.concatenate([wi[:, :o_kr], wi[:, o_pool:], _rope_slot(wi[:, o_kr:o_pool])], axis=1)
    wq = w_uq[l].reshape(Q_RANK, N_HEADS, NOPE_DIM + ROPE_DIM)
    wuq = jnp.concatenate(
        [wq[:, :, :NOPE_DIM].reshape(Q_RANK, N_HEADS * NOPE_DIM)]
        + [_rope_slot(wq[:, hh, NOPE_DIM:]) for hh in range(N_HEADS)], axis=1)
    row = lambda v: v.reshape(1, -1)
    return {
        'norm_pre': row(norm_pre[l]), 'w_in': win.astype(bf16), 'q_norm': row(q_norm[l]),
        'w_uq': wuq.astype(bf16), 'kv_norm': row(kv_norm[l]),
        'w_uk': _block_diag(jnp.transpose(w_uk[l], (1, 2, 0))).astype(bf16),
        'w_uv': _block_diag(jnp.transpose(w_uv[l], (1, 0, 2))).astype(bf16),
        'pool_w': _block_diag(pool_w[l]).astype(bf16), 'pool_scale': row(pool_scale[l]),
        'conv_dw': conv_dw[l], 'conv_b': row(conv_b[l]), 'conv_ln_g': row(conv_ln_g[l]),
        'conv_ln_b': row(conv_ln_b[l]), 'conv_pw': conv_pw[l].astype(bf16),
        'w_out': w_out[l].astype(bf16), 'norm_post': row(norm_post[l]),
    }


def _rope_table(pos):
    inv = ROPE_THETA ** (-jnp.arange(0, ROPE_DIM, 2, dtype=f32) / ROPE_DIM)
    ang = pos.astype(f32)[:, None] * inv[None, :]
    cos, sin = jnp.cos(ang), jnp.sin(ang)
    pad = jnp.zeros((pos.shape[0], LANES - 2 * ROPE_DIM), f32)
    return jnp.concatenate([cos, cos, -sin, sin, pad], axis=1)


def kernel(x_prompt, x_sample, cache_ckv, cache_krope, page_table, state_pool, state_conv, norm_pre, w_in,
           q_norm, w_uq, kv_norm, w_uk, w_uv, pool_w, pool_scale, conv_dw, conv_b, conv_ln_g, conv_ln_b,
           conv_pw, w_out, norm_post):
    bp, lp, _ = x_prompt.shape
    bs, ls, _ = x_sample.shape
    assert bp == 1
    depth = w_in.shape[0]
    past = page_table.shape[1] * PAGE_SIZE
    rows_s = bs * ls

    t1_p = _rope_table(jnp.arange(lp, dtype=jnp.int32))
    t1_s = _rope_table(past + jnp.repeat(jnp.arange(ls, dtype=jnp.int32), bs))
    hp = x_prompt.reshape(lp, D_MODEL)
    hs = jnp.transpose(x_sample, (1, 0, 2)).reshape(rows_s, D_MODEL)
    zero_hist = jnp.zeros((32, POOL_W), f32)
    cache_krope_t = jnp.swapaxes(cache_krope, 2, 3)

    def from_time_major(a, width):
        return jnp.transpose(a.reshape(ls, bs, width), (1, 0, 2))

    outs = {k: [] for k in ('ckv_p', 'kr_p', 'pool_p', 'conv_p', 'ckv_s', 'kr_s', 'pool_s', 'conv_s')}
    for l in range(depth):
        w = _prep_layer(l, norm_pre, w_in, q_norm, w_uq, kv_norm, w_uk, w_uv, pool_w, pool_scale,
                        conv_dw, conv_b, conv_ln_g, conv_ln_b, conv_pw, w_out, norm_post)
        q, kcat, ckv, kr, mix, gatt, ptail, ctail = _pre_call(
            hp, t1_p, zero_hist, zero_hist, w, tm=512, stride=1, pos0=0)
        olat = _attn_prompt_call(q, kcat, tq=256, tk=512)
        hp = _post_call(hp, olat, gatt, mix, w, tm=512)
        outs['ckv_p'].append(ckv.reshape(bp, lp, KV_RANK))
        outs['kr_p'].append(kr.reshape(bp, lp, ROPE_DIM))
        outs['pool_p'].append(ptail[-POOL_HIST:].reshape(bp, POOL_HIST, POOL_W))
        outs['conv_p'].append(ctail[-CONV_HIST:].reshape(bp, CONV_HIST, CONV_W))
        hist_pool = jnp.transpose(state_pool[l], (1, 0, 2)).reshape(POOL_HIST * bs, POOL_W)
        hist_conv = jnp.transpose(state_conv[l], (1, 0, 2)).reshape(CONV_HIST * bs, CONV_W)
        q, kcat, ckv, kr, mix, gatt, ptail, ctail = _pre_call(
            hs, t1_s, hist_pool, hist_conv, w, tm=rows_s, stride=bs, pos0=past)
        q_b = jnp.transpose(q.reshape(N_HEADS, ls, bs, KCAT_W), (2, 0, 1, 3)).reshape(bs, N_HEADS * ls, KCAT_W)
        knew = jnp.pad(from_time_major(kcat, KCAT_W), ((0, 0), (0, 16 - ls), (0, 0)))
        o_b = _attn_sample_call(page_table, q_b, knew, cache_ckv, cache_krope_t, layer=l)
        olat = jnp.transpose(o_b.reshape(bs, N_HEADS, ls, KV_RANK), (1, 2, 0, 3)).reshape(N_HEADS, rows_s, KV_RANK)
        hs = _post_call(hs, olat, gatt, mix, w, tm=rows_s)
        outs['ckv_s'].append(from_time_major(ckv, KV_RANK))
        outs['kr_s'].append(from_time_major(kr, ROPE_DIM))
        outs['pool_s'].append(jnp.concatenate([state_pool[l][:, ls:], from_time_major(ptail, POOL_W)], axis=1))
        outs['conv_s'].append(jnp.concatenate([state_conv[l][:, ls:], from_time_major(ctail, CONV_W)], axis=1))

    y_prompt = hp.reshape(bp, lp, D_MODEL)
    y_sample = from_time_major(hs, D_MODEL)
    return (y_prompt, y_sample,
            jnp.stack(outs['ckv_p']), jnp.stack(outs['kr_p']), jnp.stack(outs['pool_p']), jnp.stack(outs['conv_p']),
            jnp.stack(outs['ckv_s']), jnp.stack(outs['kr_s']), jnp.stack(outs['pool_s']), jnp.stack(outs['conv_s']))
```

```python
import functools
import math

import jax
import jax.numpy as jnp
from jax import lax
from jax.experimental import pallas as pl
from jax.experimental.pallas import tpu as pltpu

D_MODEL = 1024
N_HEADS = 8
NOPE_DIM = 64
ROPE_DIM = 32
V_DIM = 64
Q_RANK = 256
KV_RANK = 128
ATT_W = N_HEADS * V_DIM
POOL_W = 256
POOL_GW = 64
POOL_HIST = 15
CONV_W = 256
CONV_K = 31
CONV_HIST = CONV_K - 1
PAGE_SIZE = 128
ATT_SCALE = (NOPE_DIM + ROPE_DIM) ** -0.5
Q_SCALE = ATT_SCALE * math.log2(math.e)
ROPE_THETA = 10000.0
EPS = 1e-6

LANES = 128
SUBLANES = 8
KCAT_W = 2 * LANES
ONES_COL = KV_RANK + ROPE_DIM
NEG = -1e30

C_Q, C_KV, C_POOL, C_CONV, C_GATE, C_KR, C_END = 0, 256, 384, 640, 1152, 2176, 2304
U_ROPE, U_END = N_HEADS * NOPE_DIM, N_HEADS * NOPE_DIM + N_HEADS * LANES

VMEM_LIMIT = 56 * 1024 * 1024

f32 = jnp.float32
bf16 = jnp.bfloat16


def _dot(a, b):
    return jnp.dot(a, b, preferred_element_type=f32)


def _dot_nt(a, b):
    return lax.dot_general(a, b, (((1,), (1,)), ((), ())), preferred_element_type=f32)


def _rms(x, g):
    return x * lax.rsqrt(jnp.mean(x * x, axis=-1, keepdims=True) + EPS) * g


def _sigmoid(x):
    return 1.0 / (1.0 + jnp.exp(-x))


def _align8(n):
    return (n + 7) // 8 * 8


def _pre_kernel(x_ref, t1_ref, hp_ref, hc_ref, npre_ref, win_ref, qn_ref, wuq_ref, kvn_ref, wuk_ref,
                poolw_ref, pscale_ref, dw_ref, db_ref, lng_ref, lnb_ref, pw_ref,
                q_ref, kcat_ref, ckv_ref, kr_ref, mix_ref, gatt_ref, ptail_ref, ctail_ref,
                pext, cext, buf2, buf4, buf8, ybuf, cshift, *, tm, stride, ph, ch, pos0, tail, carry):
    i = pl.program_id(0)

    @pl.when(i == 0)
    def _():
        pext[0:ph, :] = hp_ref[...]
        cext[0:ch, :] = hc_ref[...]

    h = _rms(x_ref[...], npre_ref[...]).astype(bf16)
    t1 = t1_ref[...]
    lane = lax.broadcasted_iota(jnp.int32, (tm, LANES), 1)
    n = ph + tm

    def rope(slot):
        pr = slot * t1
        return pr + pltpu.roll(pr, LANES - ROPE_DIM, 1)

    uc = _dot(h, win_ref[:, C_CONV:C_GATE])
    cext[ch:ch + tm, :] = uc[:, 0:CONV_W] * _sigmoid(uc[:, CONV_W:2 * CONV_W])
    pext[ph:n, :] = _dot(h, win_ref[:, C_POOL:C_CONV])
    shifted = stride % SUBLANES != 0
    if shifted:
        for j in range(1, SUBLANES):
            cshift[j - 1, SUBLANES:ch + tm, :] = cext[SUBLANES - j:ch + tm - j, :]
    rc = tm // 4

    def conv_rows(r0):
        acc = jnp.broadcast_to(db_ref[...], (rc, CONV_W))
        for k in range(CONV_K):
            back = (CONV_HIST - k) * stride
            j = back % SUBLANES if shifted else 0
            off = ch + r0 - (back - j)
            rows = cext[off:off + rc, :] if j == 0 else cshift[j - 1, off:off + rc, :]
            acc = acc + rows * dw_ref[k:k + 1, :]
        ybuf[r0:r0 + rc, :] = acc

    conv_rows(0)
    cq = _rms(_dot(h, win_ref[:, C_Q:C_KV]), qn_ref[...]).astype(bf16)
    qz = _dot(cq, wuq_ref[...])
    qabs = _dot(qz[:, 0:U_ROPE].astype(bf16), wuk_ref[...])
    for hh in range(N_HEADS):
        q_ref[hh, :, 0:LANES] = (qabs[:, hh * LANES:(hh + 1) * LANES] * Q_SCALE).astype(bf16)
        qr = rope(qz[:, U_ROPE + hh * LANES:U_ROPE + (hh + 1) * LANES])
        q_ref[hh, :, LANES:KCAT_W] = jnp.where(lane < ROPE_DIM, qr * Q_SCALE, 0.0).astype(bf16)

    conv_rows(rc)
    ckv = _rms(_dot(h, win_ref[:, C_KV:C_POOL]), kvn_ref[...])
    ckv_ref[...] = ckv
    kr = rope(_dot(h, win_ref[:, C_KR:C_END]))
    kr_ref[...] = kr[:, 0:ROPE_DIM]
    kcat_ref[:, 0:LANES] = ckv.astype(bf16)
    kcat_ref[:, LANES:KCAT_W] = jnp.where(
        lane < ROPE_DIM, kr, jnp.where(lane == ROPE_DIM, 1.0, 0.0)).astype(bf16)

    conv_rows(2 * rc)
    sgate = _dot(h, win_ref[:, C_GATE:C_KR])
    sgate = sgate * _sigmoid(sgate)
    gatt_ref[...] = sgate[:, 0:ATT_W]
    conv_rows(3 * rc)

    lo2 = _align8(stride)
    lo4 = _align8(lo2 + 2 * stride)
    lo8 = _align8(lo4 + 4 * stride)
    buf2[lo2:n, :] = pext[lo2:n, :] + pext[lo2 - stride:n - stride, :]
    buf4[lo4:n, :] = buf2[lo4:n, :] + buf2[lo4 - 2 * stride:n - 2 * stride, :]
    buf8[lo8:n, :] = buf4[lo8:n, :] + buf4[lo8 - 4 * stride:n - 4 * stride, :]
    s16 = buf8[ph:n, :] + buf8[ph - 8 * stride:n - 8 * stride, :]
    col = lax.broadcasted_iota(jnp.int32, (tm, POOL_W), 1)
    row = lax.broadcasted_iota(jnp.int32, (tm, POOL_W), 0)
    wsum = jnp.where(col < POOL_GW, buf2[ph:n, :],
                     jnp.where(col < 2 * POOL_GW, buf4[ph:n, :],
                               jnp.where(col < 3 * POOL_GW, buf8[ph:n, :], s16)))
    win = jnp.where(col < POOL_GW, 2, jnp.where(col < 2 * POOL_GW, 4, jnp.where(col < 3 * POOL_GW, 8, 16)))
    pos = pos0 + (i * tm + row) // stride
    cnt = jnp.minimum(pos + 1, win).astype(f32)
    d = wsum / cnt - pext[ph:n, :]
    ypool = _dot(d.astype(bf16), poolw_ref[...]) * pscale_ref[...]
    mix_ref[:, 0:POOL_W] = (ypool * sgate[:, ATT_W:ATT_W + POOL_W]).astype(bf16)

    y = ybuf[...]
    mu = jnp.mean(y, axis=-1, keepdims=True)
    yc = y - mu
    var = jnp.mean(yc * yc, axis=-1, keepdims=True)
    yn = yc * lax.rsqrt(var + EPS) * lng_ref[...] + lnb_ref[...]
    yconv = _dot((yn * _sigmoid(yn)).astype(bf16), pw_ref[...])
    mix_ref[:, POOL_W:POOL_W + CONV_W] = (yconv * sgate[:, ATT_W + POOL_W:]).astype(bf16)

    ptail_ref[...] = pext[n - tail:n, :]
    ctail_ref[...] = cext[ch + tm - tail:ch + tm, :]
    if carry:
        pext[0:ph, :] = pext[tm:tm + ph, :]
        cext[0:ch, :] = cext[tm:tm + ch, :]


def _pre_call(x, t1, hist_pool, hist_conv, w, *, tm, stride, pos0):
    rows = x.shape[0]
    ph, ch = hist_pool.shape[0], hist_conv.shape[0]
    nt = rows // tm
    tail = min(tm, _align8(CONV_HIST * stride))
    kern = functools.partial(_pre_kernel, tm=tm, stride=stride, ph=ph, ch=ch, pos0=pos0, tail=tail,
                             carry=nt > 1)
    rowblk = lambda width: pl.BlockSpec((tm, width), lambda i: (i, 0))
    full = lambda a: pl.BlockSpec(a.shape, lambda i: (0,) * a.ndim)
    weights = (w['norm_pre'], w['w_in'], w['q_norm'], w['w_uq'], w['kv_norm'], w['w_uk'], w['pool_w'],
               w['pool_scale'], w['conv_dw'], w['conv_b'], w['conv_ln_g'], w['conv_ln_b'], w['conv_pw'])
    return pl.pallas_call(
        kern,
        grid=(nt,),
        in_specs=[rowblk(D_MODEL), rowblk(LANES), full(hist_pool), full(hist_conv)] + [full(a) for a in weights],
        out_specs=[
            pl.BlockSpec((N_HEADS, tm, KCAT_W), lambda i: (0, i, 0)),
            rowblk(KCAT_W), rowblk(KV_RANK), rowblk(ROPE_DIM), rowblk(POOL_W + CONV_W), rowblk(ATT_W),
            pl.BlockSpec((tail, POOL_W), lambda i: (0, 0)),
            pl.BlockSpec((tail, CONV_W), lambda i: (0, 0)),
        ],
        out_shape=[
            jax.ShapeDtypeStruct((N_HEADS, rows, KCAT_W), bf16),
            jax.ShapeDtypeStruct((rows, KCAT_W), bf16),
            jax.ShapeDtypeStruct((rows, KV_RANK), f32),
            jax.ShapeDtypeStruct((rows, ROPE_DIM), f32),
            jax.ShapeDtypeStruct((rows, POOL_W + CONV_W), bf16),
            jax.ShapeDtypeStruct((rows, ATT_W), f32),
            jax.ShapeDtypeStruct((tail, POOL_W), f32),
            jax.ShapeDtypeStruct((tail, CONV_W), f32),
        ],
        scratch_shapes=[
            pltpu.VMEM((ph + tm, POOL_W), f32), pltpu.VMEM((ch + tm, CONV_W), f32),
            pltpu.VMEM((ph + tm, POOL_W), f32), pltpu.VMEM((ph + tm, POOL_W), f32),
            pltpu.VMEM((ph + tm, POOL_W), f32), pltpu.VMEM((tm, CONV_W), f32),
            pltpu.VMEM((SUBLANES - 1, ch + tm if stride % SUBLANES else SUBLANES, CONV_W), f32),
        ],
        compiler_params=pltpu.CompilerParams(dimension_semantics=("arbitrary",), vmem_limit_bytes=VMEM_LIMIT),
        name="pre",
    )(x, t1, hist_pool, hist_conv, *weights)


def _attn_prompt_kernel(q_ref, qnext_ref, k_ref, o_ref, m_scr, acc_scr, s_a, s_b, *, tq):
    g = pl.program_id(0)
    tk = 2 * tq
    m_rows = N_HEADS * tq

    def key_tile(t):
        return k_ref[pl.ds(pl.multiple_of(t * tk, tk), tk), :]

    def scores(q, t, dst):
        dst[...] = _dot_nt(q.reshape(m_rows, KCAT_W), key_tile(t))

    def q_sub(sub):
        return q_ref[:, sub * tq:(sub + 1) * tq, :]

    def consume(src, sub, t, masked):
        s = src[...]
        if masked:
            qoff = sub * tq + (lax.broadcasted_iota(jnp.int32, (m_rows, tk), 0) & (tq - 1))
            s = jnp.where(lax.broadcasted_iota(jnp.int32, (m_rows, tk), 1) <= qoff, s, NEG)
        m_prev = m_scr[sub]
        m_new = jnp.maximum(m_prev, jnp.max(s, axis=1, keepdims=True))
        alpha = jnp.exp2(m_prev - m_new)
        p = jnp.exp2(s - jnp.concatenate([m_new] * (tk // LANES), axis=1))
        pv = _dot(p.astype(bf16), key_tile(t))
        acc_scr[sub] = acc_scr[sub] * jnp.concatenate([alpha] * (KCAT_W // LANES), axis=1) + pv
        m_scr[sub] = m_new

    def reset():
        m_scr[...] = jnp.full(m_scr.shape, NEG, f32)
        acc_scr[...] = jnp.zeros(acc_scr.shape, f32)

    @pl.when(g == 0)
    def _():
        reset()
        scores(q_sub(0), 0, s_a)

    def pair(t, c):
        scores(q_sub(1), t, s_b)
        consume(s_a, 0, t, False)
        scores(q_sub(0), t + 1, s_a)
        consume(s_b, 1, t, False)
        return c

    def quad(j, c):
        pair(2 * j, c)
        return pair(2 * j + 1, c)

    lax.fori_loop(0, g // 2, quad, 0)
    lax.fori_loop(2 * (g // 2), g, pair, 0)

    scores(q_sub(1), g, s_b)
    consume(s_a, 0, g, True)
    scores(qnext_ref[...], 0, s_a)
    consume(s_b, 1, g, True)
    for sub in range(2):
        acc = acc_scr[sub]
        out = acc[:, 0:KV_RANK] / acc[:, ONES_COL:ONES_COL + 1]
        o_ref[:, sub * tq:(sub + 1) * tq, :] = out.astype(bf16).reshape(N_HEADS, tq, KV_RANK)
    reset()


def _attn_prompt_call(q, kcat, *, tq):
    rows = kcat.shape[0]
    tk = 2 * tq
    nsub = rows // tq
    kern = functools.partial(_attn_prompt_kernel, tq=tq)
    return pl.pallas_call(
        kern,
        grid=(rows // tk,),
        in_specs=[pl.BlockSpec((N_HEADS, tk, KCAT_W), lambda g: (0, g, 0)),
                  pl.BlockSpec((N_HEADS, tq, KCAT_W), lambda g: (0, jnp.minimum(2 * g + 2, nsub - 1), 0)),
                  pl.BlockSpec((rows, KCAT_W), lambda g: (0, 0))],
        out_specs=pl.BlockSpec((N_HEADS, tk, KV_RANK), lambda g: (0, g, 0)),
        out_shape=jax.ShapeDtypeStruct((N_HEADS, rows, KV_RANK), bf16),
        scratch_shapes=[pltpu.VMEM((2, N_HEADS * tq, LANES), f32), pltpu.VMEM((2, N_HEADS * tq, KCAT_W), f32),
                        pltpu.VMEM((N_HEADS * tq, tk), f32), pltpu.VMEM((N_HEADS * tq, tk), f32)],
        compiler_params=pltpu.CompilerParams(dimension_semantics=("arbitrary",), vmem_limit_bytes=VMEM_LIMIT),
        name="attn_prompt",
    )(q, q, kcat)


def _attn_sample_kernel(pt_ref, q_ref, knew_ref, ckv_hbm, krt_hbm, o_ref, kbuf, rbuf, sem,
                        *, layer, n_batch, n_pages, n_new):
    b = pl.program_id(0)
    slot = b % 2

    def page_copies(bb, sl, p):
        page = pt_ref[bb, p]
        dst = pl.ds(p * PAGE_SIZE, PAGE_SIZE)
        return (pltpu.make_async_copy(ckv_hbm.at[layer, page], kbuf.at[sl, dst], sem.at[0, sl]),
                pltpu.make_async_copy(krt_hbm.at[layer, page], rbuf.at[sl, :, dst], sem.at[1, sl]))

    def issue(bb, sl):
        for p in range(n_pages):
            for cpy in page_copies(bb, sl, p):
                cpy.start()

    @pl.when(b == 0)
    def _():
        issue(0, 0)

    @pl.when(b + 1 < n_batch)
    def _():
        issue(b + 1, 1 - slot)

    for p in range(n_pages):
        for cpy in page_copies(b, slot, p):
            cpy.wait()

    q = q_ref[0]
    kn = knew_ref[0]
    kb = kbuf[slot].astype(bf16)
    rb = rbuf[slot].astype(bf16)
    s = _dot_nt(q[:, 0:KV_RANK], kb) + _dot(q[:, KV_RANK:KV_RANK + ROPE_DIM], rb)
    sn = _dot_nt(q, kn)
    t_q = lax.broadcasted_iota(jnp.int32, sn.shape, 0) % n_new
    t_k = lax.broadcasted_iota(jnp.int32, sn.shape, 1)
    sn = jnp.where(t_k <= t_q, sn, NEG)
    m = jnp.maximum(jnp.max(s, axis=1, keepdims=True), jnp.max(sn, axis=1, keepdims=True))
    p = jnp.exp2(s - m)
    pn = jnp.exp2(sn - m)
    l = jnp.sum(p, axis=1, keepdims=True) + jnp.sum(pn, axis=1, keepdims=True)
    o = _dot(p.astype(bf16), kb) + _dot(pn.astype(bf16), kn[:, 0:KV_RANK])
    o_ref[0] = (o / l).astype(bf16)


def _attn_sample_call(page_table, q, knew, cache_ckv, cache_krope_t, *, layer):
    n_batch, n_pages = page_table.shape
    past = n_pages * PAGE_SIZE
    rows_q = q.shape[1]
    kern = functools.partial(_attn_sample_kernel, layer=layer, n_batch=n_batch, n_pages=n_pages,
                             n_new=rows_q // N_HEADS)
    grid_spec = pltpu.PrefetchScalarGridSpec(
        num_scalar_prefetch=1,
        grid=(n_batch,),
        in_specs=[pl.BlockSpec((1, rows_q, KCAT_W), lambda b, pt: (b, 0, 0)),
                  pl.BlockSpec((1,) + knew.shape[1:], lambda b, pt: (b, 0, 0)),
                  pl.BlockSpec(memory_space=pl.ANY),
                  pl.BlockSpec(memory_space=pl.ANY)],
        out_specs=pl.BlockSpec((1, rows_q, KV_RANK), lambda b, pt: (b, 0, 0)),
        scratch_shapes=[pltpu.VMEM((2, past, KV_RANK), f32),
                        pltpu.VMEM((2, ROPE_DIM, past), f32),
                        pltpu.SemaphoreType.DMA((2, 2))],
    )
    return pl.pallas_call(
        kern,
        grid_spec=grid_spec,
        out_shape=jax.ShapeDtypeStruct((n_batch, rows_q, KV_RANK), bf16),
        compiler_params=pltpu.CompilerParams(dimension_semantics=("arbitrary",), vmem_limit_bytes=VMEM_LIMIT),
        name="attn_sample",
    )(page_table, q, knew, cache_ckv, cache_krope_t)


def _post_kernel(x_ref, o_ref, gatt_ref, mix_ref, wuv_ref, wout_ref, npost_ref, y_ref):
    olat = jnp.concatenate([o_ref[hh] for hh in range(N_HEADS)], axis=1)
    oatt = _dot(olat, wuv_ref[...])
    mixed = jnp.concatenate([(oatt * gatt_ref[...]).astype(bf16), mix_ref[...]], axis=1)
    out = _dot(mixed, wout_ref[...])
    y_ref[...] = x_ref[...] + _rms(out, npost_ref[...])


def _post_call(x, olat, gatt, mix, w, *, tm):
    rows = x.shape[0]
    rowblk = lambda width: pl.BlockSpec((tm, width), lambda i: (i, 0))
    full = lambda a: pl.BlockSpec(a.shape, lambda i: (0,) * a.ndim)
    return pl.pallas_call(
        _post_kernel,
        grid=(rows // tm,),
        in_specs=[rowblk(D_MODEL), pl.BlockSpec((N_HEADS, tm, KV_RANK), lambda i: (0, i, 0)),
                  rowblk(ATT_W), rowblk(POOL_W + CONV_W), full(w['w_uv']), full(w['w_out']), full(w['norm_post'])],
        out_specs=rowblk(D_MODEL),
        out_shape=jax.ShapeDtypeStruct((rows, D_MODEL), f32),
        compiler_params=pltpu.CompilerParams(dimension_semantics=("arbitrary",), vmem_limit_bytes=VMEM_LIMIT),
        name="post",
    )(x, olat, gatt, mix, w['w_uv'], w['w_out'], w['norm_post'])


def _swap_halves(wcols):
    half = wcols.shape[-1] // 2
    return jnp.concatenate([wcols[..., half:], wcols[..., :half]], axis=-1)


def _rope_slot(wcols):
    pad = jnp.zeros((wcols.shape[0], LANES - 2 * ROPE_DIM), wcols.dtype)
    return jnp.concatenate([wcols, _swap_halves(wcols), pad], axis=-1)


def _block_diag(blocks):
    g, a, b = blocks.shape
    eye = jnp.eye(g, dtype=blocks.dtype)
    return jnp.einsum('gab,gh->gahb', blocks, eye).reshape(g * a, g * b)


def _prep_layer(l, norm_pre, w_in, q_norm, w_uq, kv_norm, w_uk, w_uv, pool_w, pool_scale,
                conv_dw, conv_b, conv_ln_g, conv_ln_b, conv_pw, w_out, norm_post):
    wi = w_in[l]
    o_kr = Q_RANK + KV_RANK
    o_pool = o_kr + ROPE_DIM
    win = jnp.concatenate([wi[:, :o_kr], wi[:, o_pool:], _rope_slot(wi[:, o_kr:o_pool])], axis=1)
    wq = w_uq[l].reshape(Q_RANK, N_HEADS, NOPE_DIM + ROPE_DIM)
    wuq = jnp.concatenate(
        [wq[:, :, :NOPE_DIM].reshape(Q_RANK, N_HEADS * NOPE_DIM)]
        + [_rope_slot(wq[:, hh, NOPE_DIM:]) for hh in range(N_HEADS)], axis=1)
    row = lambda v: v.reshape(1, -1)
    return {
        'norm_pre': row(norm_pre[l]), 'w_in': win.astype(bf16), 'q_norm': row(q_norm[l]),
        'w_uq': wuq.astype(bf16), 'kv_norm': row(kv_norm[l]),
        'w_uk': _block_diag(jnp.transpose(w_uk[l], (1, 2, 0))).astype(bf16),
        'w_uv': _block_diag(jnp.transpose(w_uv[l], (1, 0, 2))).astype(bf16),
        'pool_w': _block_diag(pool_w[l]).astype(bf16), 'pool_scale': row(pool_scale[l]),
        'conv_dw': conv_dw[l], 'conv_b': row(conv_b[l]), 'conv_ln_g': row(conv_ln_g[l]),
        'conv_ln_b': row(conv_ln_b[l]), 'conv_pw': conv_pw[l].astype(bf16),
        'w_out': w_out[l].astype(bf16), 'norm_post': row(norm_post[l]),
    }


def _rope_table(pos):
    inv = ROPE_THETA ** (-jnp.arange(0, ROPE_DIM, 2, dtype=f32) / ROPE_DIM)
    ang = pos.astype(f32)[:, None] * inv[None, :]
    cos, sin = jnp.cos(ang), jnp.sin(ang)
    pad = jnp.zeros((pos.shape[0], LANES - 2 * ROPE_DIM), f32)
    return jnp.concatenate([cos, cos, -sin, sin, pad], axis=1)


def kernel(x_prompt, x_sample, cache_ckv, cache_krope, page_table, state_pool, state_conv, norm_pre, w_in,
           q_norm, w_uq, kv_norm, w_uk, w_uv, pool_w, pool_scale, conv_dw, conv_b, conv_ln_g, conv_ln_b,
           conv_pw, w_out, norm_post):
    bp, lp, _ = x_prompt.shape
    bs, ls, _ = x_sample.shape
    assert bp == 1
    depth = w_in.shape[0]
    past = page_table.shape[1] * PAGE_SIZE
    rows_s = bs * ls

    t1_p = _rope_table(jnp.arange(lp, dtype=jnp.int32))
    t1_s = _rope_table(past + jnp.repeat(jnp.arange(ls, dtype=jnp.int32), bs))
    hp = x_prompt.reshape(lp, D_MODEL)
    hs = jnp.transpose(x_sample, (1, 0, 2)).reshape(rows_s, D_MODEL)
    zero_hist = jnp.zeros((32, POOL_W), f32)
    cache_krope_t = jnp.swapaxes(cache_krope, 2, 3)

    def from_time_major(a, width):
        return jnp.transpose(a.reshape(ls, bs, width), (1, 0, 2))

    outs = {k: [] for k in ('ckv_p', 'kr_p', 'pool_p', 'conv_p', 'ckv_s', 'kr_s', 'pool_s', 'conv_s')}
    for l in range(depth):
        w = _prep_layer(l, norm_pre, w_in, q_norm, w_uq, kv_norm, w_uk, w_uv, pool_w, pool_scale,
                        conv_dw, conv_b, conv_ln_g, conv_ln_b, conv_pw, w_out, norm_post)
        q, kcat, ckv, kr, mix, gatt, ptail, ctail = _pre_call(
            hp, t1_p, zero_hist, zero_hist, w, tm=512, stride=1, pos0=0)
        olat = _attn_prompt_call(q, kcat, tq=256)
        hp = _post_call(hp, olat, gatt, mix, w, tm=512)
        outs['ckv_p'].append(ckv.reshape(bp, lp, KV_RANK))
        outs['kr_p'].append(kr.reshape(bp, lp, ROPE_DIM))
        outs['pool_p'].append(ptail[-POOL_HIST:].reshape(bp, POOL_HIST, POOL_W))
        outs['conv_p'].append(ctail[-CONV_HIST:].reshape(bp, CONV_HIST, CONV_W))
        hist_pool = jnp.transpose(state_pool[l], (1, 0, 2)).reshape(POOL_HIST * bs, POOL_W)
        hist_conv = jnp.transpose(state_conv[l], (1, 0, 2)).reshape(CONV_HIST * bs, CONV_W)
        q, kcat, ckv, kr, mix, gatt, ptail, ctail = _pre_call(
            hs, t1_s, hist_pool, hist_conv, w, tm=rows_s, stride=bs, pos0=past)
        q_b = jnp.transpose(q.reshape(N_HEADS, ls, bs, KCAT_W), (2, 0, 1, 3)).reshape(bs, N_HEADS * ls, KCAT_W)
        knew = jnp.pad(from_time_major(kcat, KCAT_W), ((0, 0), (0, 16 - ls), (0, 0)))
        o_b = _attn_sample_call(page_table, q_b, knew, cache_ckv, cache_krope_t, layer=l)
        olat = jnp.transpose(o_b.reshape(bs, N_HEADS, ls, KV_RANK), (1, 2, 0, 3)).reshape(N_HEADS, rows_s, KV_RANK)
        hs = _post_call(hs, olat, gatt, mix, w, tm=rows_s)
        outs['ckv_s'].append(from_time_major(ckv, KV_RANK))
        outs['kr_s'].append(from_time_major(kr, ROPE_DIM))
        outs['pool_s'].append(jnp.concatenate([state_pool[l][:, ls:], from_time_major(ptail, POOL_W)], axis=1))
        outs['conv_s'].append(jnp.concatenate([state_conv[l][:, ls:], from_time_major(ctail, CONV_W)], axis=1))

    y_prompt = hp.reshape(bp, lp, D_MODEL)
    y_sample = from_time_major(hs, D_MODEL)
    return (y_prompt, y_sample,
            jnp.stack(outs['ckv_p']), jnp.stack(outs['kr_p']), jnp.stack(outs['pool_p']), jnp.stack(outs['conv_p']),
            jnp.stack(outs['ckv_s']), jnp.stack(outs['kr_s']), jnp.stack(outs['pool_s']), jnp.stack(outs['conv_s']))
```

```python
import functools
import math

import jax
import jax.numpy as jnp
from jax import lax
from jax.experimental import pallas as pl
from jax.experimental.pallas import tpu as pltpu

D_MODEL = 1024
N_HEADS = 8
NOPE_DIM = 64
ROPE_DIM = 32
V_DIM = 64
Q_RANK = 256
KV_RANK = 128
ATT_W = N_HEADS * V_DIM
POOL_W = 256
POOL_GW = 64
POOL_HIST = 15
CONV_W = 256
CONV_K = 31
CONV_HIST = CONV_K - 1
PAGE_SIZE = 128
ATT_SCALE = (NOPE_DIM + ROPE_DIM) ** -0.5
Q_SCALE = ATT_SCALE * math.log2(math.e)
ROPE_THETA = 10000.0
EPS = 1e-6

LANES = 128
SUBLANES = 8
KCAT_W = 2 * LANES
ONES_COL = KV_RANK + ROPE_DIM
VT_ROWS = KV_RANK + 16
NEG = -1e30

C_Q, C_KV, C_POOL, C_CONV, C_GATE, C_KR, C_END = 0, 256, 384, 640, 1152, 2176, 2304
U_ROPE, U_END = N_HEADS * NOPE_DIM, N_HEADS * NOPE_DIM + N_HEADS * LANES

VMEM_LIMIT = 56 * 1024 * 1024

f32 = jnp.float32
bf16 = jnp.bfloat16


def _dot(a, b):
    return jnp.dot(a, b, preferred_element_type=f32)


def _dot_nt(a, b):
    return lax.dot_general(a, b, (((1,), (1,)), ((), ())), preferred_element_type=f32)


def _rms(x, g):
    return x * lax.rsqrt(jnp.mean(x * x, axis=-1, keepdims=True) + EPS) * g


def _sigmoid(x):
    return 1.0 / (1.0 + jnp.exp(-x))


def _align8(n):
    return (n + 7) // 8 * 8


def _eye(n):
    return jnp.where(lax.broadcasted_iota(jnp.int32, (n, n), 0) == lax.broadcasted_iota(jnp.int32, (n, n), 1),
                     1.0, 0.0).astype(bf16)


def _pre_kernel(x_ref, t1_ref, hp_ref, hc_ref, npre_ref, win_ref, qn_ref, wuq_ref, kvn_ref, wuk_ref,
                poolw_ref, pscale_ref, dw_ref, db_ref, lng_ref, lnb_ref, pw_ref,
                q_ref, kcat_ref, vt_ref, ckv_ref, kr_ref, mix_ref, gatt_ref, ptail_ref, ctail_ref,
                pext, cext, buf2, buf4, buf8, ybuf, cshift, *, tm, stride, ph, ch, pos0, tail, carry):
    i = pl.program_id(0)

    @pl.when(i == 0)
    def _():
        pext[0:ph, :] = hp_ref[...]
        cext[0:ch, :] = hc_ref[...]

    h = _rms(x_ref[...], npre_ref[...]).astype(bf16)
    t1 = t1_ref[...]
    lane = lax.broadcasted_iota(jnp.int32, (tm, LANES), 1)
    n = ph + tm

    def rope(slot):
        pr = slot * t1
        return pr + pltpu.roll(pr, LANES - ROPE_DIM, 1)

    uc = _dot(h, win_ref[:, C_CONV:C_GATE])
    cext[ch:ch + tm, :] = uc[:, 0:CONV_W] * _sigmoid(uc[:, CONV_W:2 * CONV_W])
    pext[ph:n, :] = _dot(h, win_ref[:, C_POOL:C_CONV])
    shifted = stride % SUBLANES != 0
    if shifted:
        for j in range(1, SUBLANES):
            cshift[j - 1, SUBLANES:ch + tm, :] = cext[SUBLANES - j:ch + tm - j, :]
    rc = tm // 4

    def conv_rows(r0):
        acc = jnp.broadcast_to(db_ref[...], (rc, CONV_W))
        for k in range(CONV_K):
            back = (CONV_HIST - k) * stride
            j = back % SUBLANES if shifted else 0
            off = ch + r0 - (back - j)
            rows = cext[off:off + rc, :] if j == 0 else cshift[j - 1, off:off + rc, :]
            acc = acc + rows * dw_ref[k:k + 1, :]
        ybuf[r0:r0 + rc, :] = acc

    conv_rows(0)
    cq = _rms(_dot(h, win_ref[:, C_Q:C_KV]), qn_ref[...]).astype(bf16)
    qz = _dot(cq, wuq_ref[...])
    qabs = _dot(qz[:, 0:U_ROPE].astype(bf16), wuk_ref[...])
    for hh in range(N_HEADS):
        q_ref[hh, :, 0:LANES] = (qabs[:, hh * LANES:(hh + 1) * LANES] * Q_SCALE).astype(bf16)
        qr = rope(qz[:, U_ROPE + hh * LANES:U_ROPE + (hh + 1) * LANES])
        q_ref[hh, :, LANES:KCAT_W] = jnp.where(lane < ROPE_DIM, qr * Q_SCALE, 0.0).astype(bf16)

    conv_rows(rc)
    ckv = _rms(_dot(h, win_ref[:, C_KV:C_POOL]), kvn_ref[...])
    ckv_ref[...] = ckv
    kr = rope(_dot(h, win_ref[:, C_KR:C_END]))
    kr_ref[...] = kr[:, 0:ROPE_DIM]
    kcat_ref[:, 0:LANES] = ckv.astype(bf16)
    kcat_ref[:, LANES:KCAT_W] = jnp.where(
        lane < ROPE_DIM, kr, jnp.where(lane == ROPE_DIM, 1.0, 0.0)).astype(bf16)
    vt_ref[0, 0:KV_RANK, :] = _dot_nt(_eye(KV_RANK), ckv.astype(bf16)).astype(bf16)
    vt_ref[0, KV_RANK:VT_ROWS, :] = jnp.where(
        lax.broadcasted_iota(jnp.int32, (VT_ROWS - KV_RANK, tm), 0) == 0, 1.0, 0.0).astype(bf16)

    conv_rows(2 * rc)
    sgate = _dot(h, win_ref[:, C_GATE:C_KR])
    sgate = sgate * _sigmoid(sgate)
    gatt_ref[...] = sgate[:, 0:ATT_W]
    conv_rows(3 * rc)

    lo2 = _align8(stride)
    lo4 = _align8(lo2 + 2 * stride)
    lo8 = _align8(lo4 + 4 * stride)
    buf2[lo2:n, :] = pext[lo2:n, :] + pext[lo2 - stride:n - stride, :]
    buf4[lo4:n, :] = buf2[lo4:n, :] + buf2[lo4 - 2 * stride:n - 2 * stride, :]
    buf8[lo8:n, :] = buf4[lo8:n, :] + buf4[lo8 - 4 * stride:n - 4 * stride, :]
    s16 = buf8[ph:n, :] + buf8[ph - 8 * stride:n - 8 * stride, :]
    col = lax.broadcasted_iota(jnp.int32, (tm, POOL_W), 1)
    row = lax.broadcasted_iota(jnp.int32, (tm, POOL_W), 0)
    wsum = jnp.where(col < POOL_GW, buf2[ph:n, :],
                     jnp.where(col < 2 * POOL_GW, buf4[ph:n, :],
                               jnp.where(col < 3 * POOL_GW, buf8[ph:n, :], s16)))
    win = jnp.where(col < POOL_GW, 2, jnp.where(col < 2 * POOL_GW, 4, jnp.where(col < 3 * POOL_GW, 8, 16)))
    pos = pos0 + (i * tm + row) // stride
    cnt = jnp.minimum(pos + 1, win).astype(f32)
    d = wsum / cnt - pext[ph:n, :]
    ypool = _dot(d.astype(bf16), poolw_ref[...]) * pscale_ref[...]
    mix_ref[:, 0:POOL_W] = (ypool * sgate[:, ATT_W:ATT_W + POOL_W]).astype(bf16)

    y = ybuf[...]
    mu = jnp.mean(y, axis=-1, keepdims=True)
    yc = y - mu
    var = jnp.mean(yc * yc, axis=-1, keepdims=True)
    yn = yc * lax.rsqrt(var + EPS) * lng_ref[...] + lnb_ref[...]
    yconv = _dot((yn * _sigmoid(yn)).astype(bf16), pw_ref[...])
    mix_ref[:, POOL_W:POOL_W + CONV_W] = (yconv * sgate[:, ATT_W + POOL_W:]).astype(bf16)

    ptail_ref[...] = pext[n - tail:n, :]
    ctail_ref[...] = cext[ch + tm - tail:ch + tm, :]
    if carry:
        pext[0:ph, :] = pext[tm:tm + ph, :]
        cext[0:ch, :] = cext[tm:tm + ch, :]


def _pre_call(x, t1, hist_pool, hist_conv, w, *, tm, stride, pos0):
    rows = x.shape[0]
    ph, ch = hist_pool.shape[0], hist_conv.shape[0]
    nt = rows // tm
    tail = min(tm, _align8(CONV_HIST * stride))
    kern = functools.partial(_pre_kernel, tm=tm, stride=stride, ph=ph, ch=ch, pos0=pos0, tail=tail,
                             carry=nt > 1)
    rowblk = lambda width: pl.BlockSpec((tm, width), lambda i: (i, 0))
    full = lambda a: pl.BlockSpec(a.shape, lambda i: (0,) * a.ndim)
    weights = (w['norm_pre'], w['w_in'], w['q_norm'], w['w_uq'], w['kv_norm'], w['w_uk'], w['pool_w'],
               w['pool_scale'], w['conv_dw'], w['conv_b'], w['conv_ln_g'], w['conv_ln_b'], w['conv_pw'])
    return pl.pallas_call(
        kern,
        grid=(nt,),
        in_specs=[rowblk(D_MODEL), rowblk(LANES), full(hist_pool), full(hist_conv)] + [full(a) for a in weights],
        out_specs=[
            pl.BlockSpec((N_HEADS, tm, KCAT_W), lambda i: (0, i, 0)),
            rowblk(KCAT_W), pl.BlockSpec((1, VT_ROWS, tm), lambda i: (i, 0, 0)),
            rowblk(KV_RANK), rowblk(ROPE_DIM), rowblk(POOL_W + CONV_W), rowblk(ATT_W),
            pl.BlockSpec((tail, POOL_W), lambda i: (0, 0)),
            pl.BlockSpec((tail, CONV_W), lambda i: (0, 0)),
        ],
        out_shape=[
            jax.ShapeDtypeStruct((N_HEADS, rows, KCAT_W), bf16),
            jax.ShapeDtypeStruct((rows, KCAT_W), bf16),
            jax.ShapeDtypeStruct((nt, VT_ROWS, tm), bf16),
            jax.ShapeDtypeStruct((rows, KV_RANK), f32),
            jax.ShapeDtypeStruct((rows, ROPE_DIM), f32),
            jax.ShapeDtypeStruct((rows, POOL_W + CONV_W), bf16),
            jax.ShapeDtypeStruct((rows, ATT_W), f32),
            jax.ShapeDtypeStruct((tail, POOL_W), f32),
            jax.ShapeDtypeStruct((tail, CONV_W), f32),
        ],
        scratch_shapes=[
            pltpu.VMEM((ph + tm, POOL_W), f32), pltpu.VMEM((ch + tm, CONV_W), f32),
            pltpu.VMEM((ph + tm, POOL_W), f32), pltpu.VMEM((ph + tm, POOL_W), f32),
            pltpu.VMEM((ph + tm, POOL_W), f32), pltpu.VMEM((tm, CONV_W), f32),
            pltpu.VMEM((SUBLANES - 1, ch + tm if stride % SUBLANES else SUBLANES, CONV_W), f32),
        ],
        compiler_params=pltpu.CompilerParams(dimension_semantics=("arbitrary",), vmem_limit_bytes=VMEM_LIMIT),
        name="pre",
    )(x, t1, hist_pool, hist_conv, *weights)


def _attn_prompt_kernel(q_ref, qnext_ref, k_ref, vt_ref, o_ref, m_scr, acc_scr, s_a, s_b, qt_scr, *, tq):
    g = pl.program_id(0)
    tk = 2 * tq
    m_cols = N_HEADS * tq

    def transpose_q(q, slot):
        qt_scr[slot] = _dot_nt(_eye(KCAT_W), q.reshape(m_cols, KCAT_W)).astype(bf16)

    def scores(slot, t, dst):
        keys = k_ref[pl.ds(pl.multiple_of(t * tk, tk), tk), :]
        dst[...] = _dot(keys, qt_scr[slot])

    def q_sub(sub):
        return q_ref[:, sub * tq:(sub + 1) * tq, :]

    def consume(src, sub, t, masked):
        s = src[...]
        if masked:
            qoff = sub * tq + (lax.broadcasted_iota(jnp.int32, (tk, m_cols), 1) & (tq - 1))
            s = jnp.where(lax.broadcasted_iota(jnp.int32, (tk, m_cols), 0) <= qoff, s, NEG)
        m_prev = m_scr[sub]
        m_new = jnp.maximum(m_prev, jnp.max(s, axis=0, keepdims=True))
        alpha = jnp.exp2(m_prev - m_new)
        p = jnp.exp2(s - m_new)
        pv = _dot(vt_ref[t], p.astype(bf16))
        acc_scr[sub] = acc_scr[sub] * alpha + pv
        m_scr[sub] = m_new

    def reset():
        m_scr[...] = jnp.full(m_scr.shape, NEG, f32)
        acc_scr[...] = jnp.zeros(acc_scr.shape, f32)

    transpose_q(q_sub(0), 0)
    transpose_q(q_sub(1), 1)

    @pl.when(g == 0)
    def _():
        reset()
        scores(0, 0, s_a)

    def pair(t, c):
        scores(1, t, s_b)
        consume(s_a, 0, t, False)
        scores(0, t + 1, s_a)
        consume(s_b, 1, t, False)
        return c

    def quad(j, c):
        pair(2 * j, c)
        return pair(2 * j + 1, c)

    lax.fori_loop(0, g // 2, quad, 0)
    lax.fori_loop(2 * (g // 2), g, pair, 0)

    scores(1, g, s_b)
    consume(s_a, 0, g, True)
    transpose_q(qnext_ref[...], 2)
    scores(2, 0, s_a)
    consume(s_b, 1, g, True)
    for sub in range(2):
        acc = acc_scr[sub]
        out = (acc[0:KV_RANK, :] / acc[KV_RANK:KV_RANK + 1, :]).T
        o_ref[:, sub * tq:(sub + 1) * tq, :] = out.astype(bf16).reshape(N_HEADS, tq, KV_RANK)
    reset()


def _attn_prompt_call(q, kcat, vt, *, tq):
    rows = kcat.shape[0]
    tk = 2 * tq
    assert vt.shape == (rows // tk, VT_ROWS, tk)
    nsub = rows // tq
    m_cols = N_HEADS * tq
    kern = functools.partial(_attn_prompt_kernel, tq=tq)
    resident = lambda a: pl.BlockSpec(a.shape, lambda g: (0,) * a.ndim, pipeline_mode=pl.Buffered(1))
    return pl.pallas_call(
        kern,
        grid=(rows // tk,),
        in_specs=[pl.BlockSpec((N_HEADS, tk, KCAT_W), lambda g: (0, g, 0)),
                  pl.BlockSpec((N_HEADS, tq, KCAT_W), lambda g: (0, jnp.minimum(2 * g + 2, nsub - 1), 0)),
                  resident(kcat), resident(vt)],
        out_specs=pl.BlockSpec((N_HEADS, tk, KV_RANK), lambda g: (0, g, 0)),
        out_shape=jax.ShapeDtypeStruct((N_HEADS, rows, KV_RANK), bf16),
        scratch_shapes=[pltpu.VMEM((2, 1, m_cols), f32), pltpu.VMEM((2, VT_ROWS, m_cols), f32),
                        pltpu.VMEM((tk, m_cols), f32), pltpu.VMEM((tk, m_cols), f32),
                        pltpu.VMEM((3, KCAT_W, m_cols), bf16)],
        compiler_params=pltpu.CompilerParams(dimension_semantics=("arbitrary",), vmem_limit_bytes=VMEM_LIMIT),
        name="attn_prompt",
    )(q, q, kcat, vt)


def _attn_sample_kernel(pt_ref, q_ref, knew_ref, ckv_hbm, krt_hbm, o_ref, kbuf, rbuf, sem,
                        *, layer, n_batch, n_pages, n_new):
    b = pl.program_id(0)
    slot = b % 2

    def page_copies(bb, sl, p):
        page = pt_ref[bb, p]
        dst = pl.ds(p * PAGE_SIZE, PAGE_SIZE)
        return (pltpu.make_async_copy(ckv_hbm.at[layer, page], kbuf.at[sl, dst], sem.at[0, sl]),
                pltpu.make_async_copy(krt_hbm.at[layer, page], rbuf.at[sl, :, dst], sem.at[1, sl]))

    def issue(bb, sl):
        for p in range(n_pages):
            for cpy in page_copies(bb, sl, p):
                cpy.start()

    @pl.when(b == 0)
    def _():
        issue(0, 0)

    @pl.when(b + 1 < n_batch)
    def _():
        issue(b + 1, 1 - slot)

    for p in range(n_pages):
        for cpy in page_copies(b, slot, p):
            cpy.wait()

    q = q_ref[0]
    kn = knew_ref[0]
    kb = kbuf[slot].astype(bf16)
    rb = rbuf[slot].astype(bf16)
    s = _dot_nt(q[:, 0:KV_RANK], kb) + _dot(q[:, KV_RANK:KV_RANK + ROPE_DIM], rb)
    sn = _dot_nt(q, kn)
    t_q = lax.broadcasted_iota(jnp.int32, sn.shape, 0) % n_new
    t_k = lax.broadcasted_iota(jnp.int32, sn.shape, 1)
    sn = jnp.where(t_k <= t_q, sn, NEG)
    m = jnp.maximum(jnp.max(s, axis=1, keepdims=True), jnp.max(sn, axis=1, keepdims=True))
    p = jnp.exp2(s - m)
    pn = jnp.exp2(sn - m)
    l = jnp.sum(p, axis=1, keepdims=True) + jnp.sum(pn, axis=1, keepdims=True)
    o = _dot(p.astype(bf16), kb) + _dot(pn.astype(bf16), kn[:, 0:KV_RANK])
    o_ref[0] = (o / l).astype(bf16)


def _attn_sample_call(page_table, q, knew, cache_ckv, cache_krope_t, *, layer):
    n_batch, n_pages = page_table.shape
    past = n_pages * PAGE_SIZE
    rows_q = q.shape[1]
    kern = functools.partial(_attn_sample_kernel, layer=layer, n_batch=n_batch, n_pages=n_pages,
                             n_new=rows_q // N_HEADS)
    grid_spec = pltpu.PrefetchScalarGridSpec(
        num_scalar_prefetch=1,
        grid=(n_batch,),
        in_specs=[pl.BlockSpec((1, rows_q, KCAT_W), lambda b, pt: (b, 0, 0)),
                  pl.BlockSpec((1,) + knew.shape[1:], lambda b, pt: (b, 0, 0)),
                  pl.BlockSpec(memory_space=pl.ANY),
                  pl.BlockSpec(memory_space=pl.ANY)],
        out_specs=pl.BlockSpec((1, rows_q, KV_RANK), lambda b, pt: (b, 0, 0)),
        scratch_shapes=[pltpu.VMEM((2, past, KV_RANK), f32),
                        pltpu.VMEM((2, ROPE_DIM, past), f32),
                        pltpu.SemaphoreType.DMA((2, 2))],
    )
    return pl.pallas_call(
        kern,
        grid_spec=grid_spec,
        out_shape=jax.ShapeDtypeStruct((n_batch, rows_q, KV_RANK), bf16),
        compiler_params=pltpu.CompilerParams(dimension_semantics=("arbitrary",), vmem_limit_bytes=VMEM_LIMIT),
        name="attn_sample",
    )(page_table, q, knew, cache_ckv, cache_krope_t)


def _post_kernel(x_ref, o_ref, gatt_ref, mix_ref, wuv_ref, wout_ref, npost_ref, y_ref):
    olat = jnp.concatenate([o_ref[hh] for hh in range(N_HEADS)], axis=1)
    oatt = _dot(olat, wuv_ref[...])
    mixed = jnp.concatenate([(oatt * gatt_ref[...]).astype(bf16), mix_ref[...]], axis=1)
    out = _dot(mixed, wout_ref[...])
    y_ref[...] = x_ref[...] + _rms(out, npost_ref[...])


def _post_call(x, olat, gatt, mix, w, *, tm):
    rows = x.shape[0]
    rowblk = lambda width: pl.BlockSpec((tm, width), lambda i: (i, 0))
    full = lambda a: pl.BlockSpec(a.shape, lambda i: (0,) * a.ndim)
    return pl.pallas_call(
        _post_kernel,
        grid=(rows // tm,),
        in_specs=[rowblk(D_MODEL), pl.BlockSpec((N_HEADS, tm, KV_RANK), lambda i: (0, i, 0)),
                  rowblk(ATT_W), rowblk(POOL_W + CONV_W), full(w['w_uv']), full(w['w_out']), full(w['norm_post'])],
        out_specs=rowblk(D_MODEL),
        out_shape=jax.ShapeDtypeStruct((rows, D_MODEL), f32),
        compiler_params=pltpu.CompilerParams(dimension_semantics=("arbitrary",), vmem_limit_bytes=VMEM_LIMIT),
        name="post",
    )(x, olat, gatt, mix, w['w_uv'], w['w_out'], w['norm_post'])


def _swap_halves(wcols):
    half = wcols.shape[-1] // 2
    return jnp.concatenate([wcols[..., half:], wcols[..., :half]], axis=-1)


def _rope_slot(wcols):
    pad = jnp.zeros((wcols.shape[0], LANES - 2 * ROPE_DIM), wcols.dtype)
    return jnp.concatenate([wcols, _swap_halves(wcols), pad], axis=-1)


def _block_diag(blocks):
    g, a, b = blocks.shape
    eye = jnp.eye(g, dtype=blocks.dtype)
    return jnp.einsum('gab,gh->gahb', blocks, eye).reshape(g * a, g * b)


def _prep_layer(l, norm_pre, w_in, q_norm, w_uq, kv_norm, w_uk, w_uv, pool_w, pool_scale,
                conv_dw, conv_b, conv_ln_g, conv_ln_b, conv_pw, w_out, norm_post):
    wi = w_in[l]
    o_kr = Q_RANK + KV_RANK
    o_pool = o_kr + ROPE_DIM
    win = jnp.concatenate([wi[:, :o_kr], wi[:, o_pool:], _rope_slot(wi[:, o_kr:o_pool])], axis=1)
    wq = w_uq[l].reshape(Q_RANK, N_HEADS, NOPE_DIM + ROPE_DIM)
    wuq = jnp.concatenate(
        [wq[:, :, :NOPE_DIM].reshape(Q_RANK, N_HEADS * NOPE_DIM)]
        + [_rope_slot(wq[:, hh, NOPE_DIM:]) for hh in range(N_HEADS)], axis=1)
    row = lambda v: v.reshape(1, -1)
    return {
        'norm_pre': row(norm_pre[l]), 'w_in': win.astype(bf16), 'q_norm': row(q_norm[l]),
        'w_uq': wuq.astype(bf16), 'kv_norm': row(kv_norm[l]),
        'w_uk': _block_diag(jnp.transpose(w_uk[l], (1, 2, 0))).astype(bf16),
        'w_uv': _block_diag(jnp.transpose(w_uv[l], (1, 0, 2))).astype(bf16),
        'pool_w': _block_diag(pool_w[l]).astype(bf16), 'pool_scale': row(pool_scale[l]),
        'conv_dw': conv_dw[l], 'conv_b': row(conv_b[l]), 'conv_ln_g': row(conv_ln_g[l]),
        'conv_ln_b': row(conv_ln_b[l]), 'conv_pw': conv_pw[l].astype(bf16),
        'w_out': w_out[l].astype(bf16), 'norm_post': row(norm_post[l]),
    }


def _rope_table(pos):
    inv = ROPE_THETA ** (-jnp.arange(0, ROPE_DIM, 2, dtype=f32) / ROPE_DIM)
    ang = pos.astype(f32)[:, None] * inv[None, :]
    cos, sin = jnp.cos(ang), jnp.sin(ang)
    pad = jnp.zeros((pos.shape[0], LANES - 2 * ROPE_DIM), f32)
    return jnp.concatenate([cos, cos, -sin, sin, pad], axis=1)


def kernel(x_prompt, x_sample, cache_ckv, cache_krope, page_table, state_pool, state_conv, norm_pre, w_in,
           q_norm, w_uq, kv_norm, w_uk, w_uv, pool_w, pool_scale, conv_dw, conv_b, conv_ln_g, conv_ln_b,
           conv_pw, w_out, norm_post):
    bp, lp, _ = x_prompt.shape
    bs, ls, _ = x_sample.shape
    assert bp == 1
    depth = w_in.shape[0]
    past = page_table.shape[1] * PAGE_SIZE
    rows_s = bs * ls

    t1_p = _rope_table(jnp.arange(lp, dtype=jnp.int32))
    t1_s = _rope_table(past + jnp.repeat(jnp.arange(ls, dtype=jnp.int32), bs))
    hp = x_prompt.reshape(lp, D_MODEL)
    hs = jnp.transpose(x_sample, (1, 0, 2)).reshape(rows_s, D_MODEL)
    zero_hist = jnp.zeros((32, POOL_W), f32)
    cache_krope_t = jnp.swapaxes(cache_krope, 2, 3)

    def from_time_major(a, width):
        return jnp.transpose(a.reshape(ls, bs, width), (1, 0, 2))

    outs = {k: [] for k in ('ckv_p', 'kr_p', 'pool_p', 'conv_p', 'ckv_s', 'kr_s', 'pool_s', 'conv_s')}
    for l in range(depth):
        w = _prep_layer(l, norm_pre, w_in, q_norm, w_uq, kv_norm, w_uk, w_uv, pool_w, pool_scale,
                        conv_dw, conv_b, conv_ln_g, conv_ln_b, conv_pw, w_out, norm_post)
        q, kcat, vt, ckv, kr, mix, gatt, ptail, ctail = _pre_call(
            hp, t1_p, zero_hist, zero_hist, w, tm=512, stride=1, pos0=0)
        olat = _attn_prompt_call(q, kcat, vt, tq=256)
        hp = _post_call(hp, olat, gatt, mix, w, tm=512)
        outs['ckv_p'].append(ckv.reshape(bp, lp, KV_RANK))
        outs['kr_p'].append(kr.reshape(bp, lp, ROPE_DIM))
        outs['pool_p'].append(ptail[-POOL_HIST:].reshape(bp, POOL_HIST, POOL_W))
        outs['conv_p'].append(ctail[-CONV_HIST:].reshape(bp, CONV_HIST, CONV_W))
        hist_pool = jnp.transpose(state_pool[l], (1, 0, 2)).reshape(POOL_HIST * bs, POOL_W)
        hist_conv = jnp.transpose(state_conv[l], (1, 0, 2)).reshape(CONV_HIST * bs, CONV_W)
        q, kcat, vt, ckv, kr, mix, gatt, ptail, ctail = _pre_call(
            hs, t1_s, hist_pool, hist_conv, w, tm=rows_s, stride=bs, pos0=past)
        q_b = jnp.transpose(q.reshape(N_HEADS, ls, bs, KCAT_W), (2, 0, 1, 3)).reshape(bs, N_HEADS * ls, KCAT_W)
        knew = jnp.pad(from_time_major(kcat, KCAT_W), ((0, 0), (0, 16 - ls), (0, 0)))
        o_b = _attn_sample_call(page_table, q_b, knew, cache_ckv, cache_krope_t, layer=l)
        olat = jnp.transpose(o_b.reshape(bs, N_HEADS, ls, KV_RANK), (1, 2, 0, 3)).reshape(N_HEADS, rows_s, KV_RANK)
        hs = _post_call(hs, olat, gatt, mix, w, tm=rows_s)
        outs['ckv_s'].append(from_time_major(ckv, KV_RANK))
        outs['kr_s'].append(from_time_major(kr, ROPE_DIM))
        outs['pool_s'].append(jnp.concatenate([state_pool[l][:, ls:], from_time_major(ptail, POOL_W)], axis=1))
        outs['conv_s'].append(jnp.concatenate([state_conv[l][:, ls:], from_time_major(ctail, CONV_W)], axis=1))

    y_prompt = hp.reshape(bp, lp, D_MODEL)
    y_sample = from_time_major(hs, D_MODEL)
    return (y_prompt, y_sample,
            jnp.stack(outs['ckv_p']), jnp.stack(outs['kr_p']), jnp.stack(outs['pool_p']), jnp.stack(outs['conv_p']),
            jnp.stack(outs['ckv_s']), jnp.stack(outs['kr_s']), jnp.stack(outs['pool_s']), jnp.stack(outs['conv_s']))
```

```python
import functools
import math

import jax
import jax.numpy as jnp
from jax import lax
from jax.experimental import pallas as pl
from jax.experimental.pallas import tpu as pltpu

D_MODEL = 1024
N_HEADS = 8
NOPE_DIM = 64
ROPE_DIM = 32
V_DIM = 64
Q_RANK = 256
KV_RANK = 128
ATT_W = N_HEADS * V_DIM
POOL_W = 256
POOL_GW = 64
POOL_HIST = 15
CONV_W = 256
CONV_K = 31
CONV_HIST = CONV_K - 1
PAGE_SIZE = 128
ATT_SCALE = (NOPE_DIM + ROPE_DIM) ** -0.5
Q_SCALE = ATT_SCALE * math.log2(math.e)
ROPE_THETA = 10000.0
EPS = 1e-6

LANES = 128
SUBLANES = 8
KCAT_W = 2 * LANES
ONES_COL = KV_RANK + ROPE_DIM
NEG = -1e30

C_Q, C_KV, C_POOL, C_CONV, C_GATE, C_KR, C_END = 0, 256, 384, 640, 1152, 2176, 2304
U_ROPE, U_END = N_HEADS * NOPE_DIM, N_HEADS * NOPE_DIM + N_HEADS * LANES

VMEM_LIMIT = 56 * 1024 * 1024

f32 = jnp.float32
bf16 = jnp.bfloat16


def _dot(a, b):
    return jnp.dot(a, b, preferred_element_type=f32)


def _dot_nt(a, b):
    return lax.dot_general(a, b, (((1,), (1,)), ((), ())), preferred_element_type=f32)


def _rms(x, g):
    return x * lax.rsqrt(jnp.mean(x * x, axis=-1, keepdims=True) + EPS) * g


def _sigmoid(x):
    return 1.0 / (1.0 + jnp.exp(-x))


def _align8(n):
    return (n + 7) // 8 * 8


def _pre_kernel(x_ref, t1_ref, hp_ref, hc_ref, npre_ref, win_ref, qn_ref, wuq_ref, kvn_ref, wuk_ref,
                poolw_ref, pscale_ref, dw_ref, db_ref, lng_ref, lnb_ref, pw_ref,
                q_ref, kcat_ref, ckv_ref, kr_ref, mix_ref, gatt_ref, ptail_ref, ctail_ref,
                pext, cext, buf2, buf4, buf8, ybuf, cshift, *, tm, stride, ph, ch, pos0, tail, carry):
    i = pl.program_id(0)

    @pl.when(i == 0)
    def _():
        pext[0:ph, :] = hp_ref[...]
        cext[0:ch, :] = hc_ref[...]

    h = _rms(x_ref[...], npre_ref[...]).astype(bf16)
    t1 = t1_ref[...]
    lane = lax.broadcasted_iota(jnp.int32, (tm, LANES), 1)
    n = ph + tm

    def rope(slot):
        pr = slot * t1
        return pr + pltpu.roll(pr, LANES - ROPE_DIM, 1)

    uc = _dot(h, win_ref[:, C_CONV:C_GATE])
    cext[ch:ch + tm, :] = uc[:, 0:CONV_W] * _sigmoid(uc[:, CONV_W:2 * CONV_W])
    pext[ph:n, :] = _dot(h, win_ref[:, C_POOL:C_CONV])
    shifted = stride % SUBLANES != 0
    if shifted:
        for j in range(1, SUBLANES):
            cshift[j - 1, SUBLANES:ch + tm, :] = cext[SUBLANES - j:ch + tm - j, :]
    rc = tm // 4

    def conv_rows(r0):
        acc = jnp.broadcast_to(db_ref[...], (rc, CONV_W))
        for k in range(CONV_K):
            back = (CONV_HIST - k) * stride
            j = back % SUBLANES if shifted else 0
            off = ch + r0 - (back - j)
            rows = cext[off:off + rc, :] if j == 0 else cshift[j - 1, off:off + rc, :]
            acc = acc + rows * dw_ref[k:k + 1, :]
        ybuf[r0:r0 + rc, :] = acc

    conv_rows(0)
    cq = _rms(_dot(h, win_ref[:, C_Q:C_KV]), qn_ref[...]).astype(bf16)
    qz = _dot(cq, wuq_ref[...])
    qabs = _dot(qz[:, 0:U_ROPE].astype(bf16), wuk_ref[...])
    for hh in range(N_HEADS):
        q_ref[hh, :, 0:LANES] = (qabs[:, hh * LANES:(hh + 1) * LANES] * Q_SCALE).astype(bf16)
        qr = rope(qz[:, U_ROPE + hh * LANES:U_ROPE + (hh + 1) * LANES])
        q_ref[hh, :, LANES:KCAT_W] = jnp.where(lane < ROPE_DIM, qr * Q_SCALE, 0.0).astype(bf16)

    conv_rows(rc)
    ckv = _rms(_dot(h, win_ref[:, C_KV:C_POOL]), kvn_ref[...])
    ckv_ref[...] = ckv
    kr = rope(_dot(h, win_ref[:, C_KR:C_END]))
    kr_ref[...] = kr[:, 0:ROPE_DIM]
    kcat_ref[:, 0:LANES] = ckv.astype(bf16)
    kcat_ref[:, LANES:KCAT_W] = jnp.where(
        lane < ROPE_DIM, kr, jnp.where(lane == ROPE_DIM, 1.0, 0.0)).astype(bf16)

    conv_rows(2 * rc)
    sgate = _dot(h, win_ref[:, C_GATE:C_KR])
    sgate = sgate * _sigmoid(sgate)
    gatt_ref[...] = sgate[:, 0:ATT_W]
    conv_rows(3 * rc)

    lo2 = _align8(stride)
    lo4 = _align8(lo2 + 2 * stride)
    lo8 = _align8(lo4 + 4 * stride)
    buf2[lo2:n, :] = pext[lo2:n, :] + pext[lo2 - stride:n - stride, :]
    buf4[lo4:n, :] = buf2[lo4:n, :] + buf2[lo4 - 2 * stride:n - 2 * stride, :]
    buf8[lo8:n, :] = buf4[lo8:n, :] + buf4[lo8 - 4 * stride:n - 4 * stride, :]
    s16 = buf8[ph:n, :] + buf8[ph - 8 * stride:n - 8 * stride, :]
    col = lax.broadcasted_iota(jnp.int32, (tm, POOL_W), 1)
    row = lax.broadcasted_iota(jnp.int32, (tm, POOL_W), 0)
    wsum = jnp.where(col < POOL_GW, buf2[ph:n, :],
                     jnp.where(col < 2 * POOL_GW, buf4[ph:n, :],
                               jnp.where(col < 3 * POOL_GW, buf8[ph:n, :], s16)))
    win = jnp.where(col < POOL_GW, 2, jnp.where(col < 2 * POOL_GW, 4, jnp.where(col < 3 * POOL_GW, 8, 16)))
    pos = pos0 + (i * tm + row) // stride
    cnt = jnp.minimum(pos + 1, win).astype(f32)
    d = wsum / cnt - pext[ph:n, :]
    ypool = _dot(d.astype(bf16), poolw_ref[...]) * pscale_ref[...]
    mix_ref[:, 0:POOL_W] = (ypool * sgate[:, ATT_W:ATT_W + POOL_W]).astype(bf16)

    y = ybuf[...]
    mu = jnp.mean(y, axis=-1, keepdims=True)
    yc = y - mu
    var = jnp.mean(yc * yc, axis=-1, keepdims=True)
    yn = yc * lax.rsqrt(var + EPS) * lng_ref[...] + lnb_ref[...]
    yconv = _dot((yn * _sigmoid(yn)).astype(bf16), pw_ref[...])
    mix_ref[:, POOL_W:POOL_W + CONV_W] = (yconv * sgate[:, ATT_W + POOL_W:]).astype(bf16)

    ptail_ref[...] = pext[n - tail:n, :]
    ctail_ref[...] = cext[ch + tm - tail:ch + tm, :]
    if carry:
        pext[0:ph, :] = pext[tm:tm + ph, :]
        cext[0:ch, :] = cext[tm:tm + ch, :]


def _pre_call(x, t1, hist_pool, hist_conv, w, *, tm, stride, pos0):
    rows = x.shape[0]
    ph, ch = hist_pool.shape[0], hist_conv.shape[0]
    nt = rows // tm
    tail = min(tm, _align8(CONV_HIST * stride))
    kern = functools.partial(_pre_kernel, tm=tm, stride=stride, ph=ph, ch=ch, pos0=pos0, tail=tail,
                             carry=nt > 1)
    rowblk = lambda width: pl.BlockSpec((tm, width), lambda i: (i, 0))
    full = lambda a: pl.BlockSpec(a.shape, lambda i: (0,) * a.ndim)
    weights = (w['norm_pre'], w['w_in'], w['q_norm'], w['w_uq'], w['kv_norm'], w['w_uk'], w['pool_w'],
               w['pool_scale'], w['conv_dw'], w['conv_b'], w['conv_ln_g'], w['conv_ln_b'], w['conv_pw'])
    return pl.pallas_call(
        kern,
        grid=(nt,),
        in_specs=[rowblk(D_MODEL), rowblk(LANES), full(hist_pool), full(hist_conv)] + [full(a) for a in weights],
        out_specs=[
            pl.BlockSpec((N_HEADS, tm, KCAT_W), lambda i: (0, i, 0)),
            rowblk(KCAT_W), rowblk(KV_RANK), rowblk(ROPE_DIM), rowblk(POOL_W + CONV_W), rowblk(ATT_W),
            pl.BlockSpec((tail, POOL_W), lambda i: (0, 0)),
            pl.BlockSpec((tail, CONV_W), lambda i: (0, 0)),
        ],
        out_shape=[
            jax.ShapeDtypeStruct((N_HEADS, rows, KCAT_W), bf16),
            jax.ShapeDtypeStruct((rows, KCAT_W), bf16),
            jax.ShapeDtypeStruct((rows, KV_RANK), f32),
            jax.ShapeDtypeStruct((rows, ROPE_DIM), f32),
            jax.ShapeDtypeStruct((rows, POOL_W + CONV_W), bf16),
            jax.ShapeDtypeStruct((rows, ATT_W), f32),
            jax.ShapeDtypeStruct((tail, POOL_W), f32),
            jax.ShapeDtypeStruct((tail, CONV_W), f32),
        ],
        scratch_shapes=[
            pltpu.VMEM((ph + tm, POOL_W), f32), pltpu.VMEM((ch + tm, CONV_W), f32),
            pltpu.VMEM((ph + tm, POOL_W), f32), pltpu.VMEM((ph + tm, POOL_W), f32),
            pltpu.VMEM((ph + tm, POOL_W), f32), pltpu.VMEM((tm, CONV_W), f32),
            pltpu.VMEM((SUBLANES - 1, ch + tm if stride % SUBLANES else SUBLANES, CONV_W), f32),
        ],
        compiler_params=pltpu.CompilerParams(dimension_semantics=("arbitrary",), vmem_limit_bytes=VMEM_LIMIT),
        name="pre",
    )(x, t1, hist_pool, hist_conv, *weights)


def _attn_prompt_kernel(q_ref, qnext_ref, k_ref, o_ref, m_scr, acc_scr, s_a, s_b, *, tq):
    g = pl.program_id(0)
    tk = 2 * tq
    m_rows = N_HEADS * tq

    def key_tile(t):
        return k_ref[pl.ds(pl.multiple_of(t * tk, tk), tk), :]

    def scores(q, t, dst):
        dst[...] = _dot_nt(q.reshape(m_rows, KCAT_W), key_tile(t))

    def q_sub(sub):
        return q_ref[:, sub * tq:(sub + 1) * tq, :]

    def consume(src, sub, t, masked):
        s = src[...]
        if masked:
            qoff = sub * tq + (lax.broadcasted_iota(jnp.int32, (m_rows, tk), 0) & (tq - 1))
            s = jnp.where(lax.broadcasted_iota(jnp.int32, (m_rows, tk), 1) <= qoff, s, NEG)
        m_prev = m_scr[sub]
        m_new = jnp.maximum(m_prev, jnp.max(s, axis=1, keepdims=True))
        alpha = jnp.exp2(m_prev - m_new)
        p = jnp.exp2(s - jnp.concatenate([m_new] * (tk // LANES), axis=1))
        pv = _dot(p.astype(bf16), key_tile(t))
        acc_scr[sub] = acc_scr[sub] * jnp.concatenate([alpha] * (KCAT_W // LANES), axis=1) + pv
        m_scr[sub] = m_new

    def reset():
        m_scr[...] = jnp.full(m_scr.shape, NEG, f32)
        acc_scr[...] = jnp.zeros(acc_scr.shape, f32)

    @pl.when(g == 0)
    def _():
        reset()
        scores(q_sub(0), 0, s_a)

    def pair(t, c):
        scores(q_sub(1), t, s_b)
        consume(s_a, 0, t, False)
        scores(q_sub(0), t + 1, s_a)
        consume(s_b, 1, t, False)
        return c

    def quad(j, c):
        pair(2 * j, c)
        return pair(2 * j + 1, c)

    lax.fori_loop(0, g // 2, quad, 0)
    lax.fori_loop(2 * (g // 2), g, pair, 0)

    scores(q_sub(1), g, s_b)
    consume(s_a, 0, g, True)
    scores(qnext_ref[...], 0, s_a)
    consume(s_b, 1, g, True)
    for sub in range(2):
        acc = acc_scr[sub]
        out = acc[:, 0:KV_RANK] / acc[:, ONES_COL:ONES_COL + 1]
        o_ref[:, sub * tq:(sub + 1) * tq, :] = out.astype(bf16).reshape(N_HEADS, tq, KV_RANK)
    reset()


def _attn_prompt_call(q, kcat, *, tq):
    rows = kcat.shape[0]
    tk = 2 * tq
    nsub = rows // tq
    kern = functools.partial(_attn_prompt_kernel, tq=tq)
    return pl.pallas_call(
        kern,
        grid=(rows // tk,),
        in_specs=[pl.BlockSpec((N_HEADS, tk, KCAT_W), lambda g: (0, g, 0)),
                  pl.BlockSpec((N_HEADS, tq, KCAT_W), lambda g: (0, jnp.minimum(2 * g + 2, nsub - 1), 0)),
                  pl.BlockSpec((rows, KCAT_W), lambda g: (0, 0))],
        out_specs=pl.BlockSpec((N_HEADS, tk, KV_RANK), lambda g: (0, g, 0)),
        out_shape=jax.ShapeDtypeStruct((N_HEADS, rows, KV_RANK), bf16),
        scratch_shapes=[pltpu.VMEM((2, N_HEADS * tq, LANES), f32), pltpu.VMEM((2, N_HEADS * tq, KCAT_W), f32),
                        pltpu.VMEM((N_HEADS * tq, tk), f32), pltpu.VMEM((N_HEADS * tq, tk), f32)],
        compiler_params=pltpu.CompilerParams(dimension_semantics=("arbitrary",), vmem_limit_bytes=VMEM_LIMIT),
        name="attn_prompt",
    )(q, q, kcat)


def _attn_sample_kernel(pt_ref, q_ref, knew_ref, ckv_hbm, krt_hbm, o_ref, kbuf, rbuf, sem,
                        *, layer, n_batch, n_pages, n_new, n_chunks):
    b = pl.program_id(0)
    slot = b % 2

    def page_copies(bb, sl, p):
        page = pt_ref[bb, p]
        dst = pl.ds(p * PAGE_SIZE, PAGE_SIZE)
        return (pltpu.make_async_copy(ckv_hbm.at[layer, page], kbuf.at[sl, dst], sem.at[0, sl]),
                pltpu.make_async_copy(krt_hbm.at[layer, page], rbuf.at[sl, :, dst], sem.at[1, sl]))

    def issue(bb, sl, pages):
        for p in pages:
            for cpy in page_copies(bb, sl, p):
                cpy.start()

    def wait_all(bb, sl):
        for p in range(n_pages):
            for cpy in page_copies(bb, sl, p):
                cpy.wait()

    @pl.when(b == 0)
    def _():
        issue(0, 0, range(n_pages))

    nxt_b = jnp.minimum(b + 1, n_batch - 1)
    burst = n_pages // n_chunks
    issue(nxt_b, 1 - slot, range(burst))
    wait_all(b, slot)

    q = q_ref[0]
    q_lat, q_rot = q[:, 0:KV_RANK], q[:, KV_RANK:KV_RANK + ROPE_DIM]
    kn = knew_ref[0]
    ck = n_pages * PAGE_SIZE // n_chunks

    def chunk_scores(c):
        kb = kbuf[slot, c * ck:(c + 1) * ck, :].astype(bf16)
        rb = rbuf[slot, :, c * ck:(c + 1) * ck].astype(bf16)
        return kb, _dot_nt(q_lat, kb) + _dot(q_rot, rb)

    def chunk_part(v, s):
        m = jnp.max(s, axis=1, keepdims=True)
        p = jnp.exp2(s - m)
        return m, jnp.sum(p, axis=1, keepdims=True), _dot(p.astype(bf16), v)

    parts = []
    cur = chunk_scores(0)
    for c in range(n_chunks):
        nxt = chunk_scores(c + 1) if c + 1 < n_chunks else None
        parts.append(chunk_part(*cur))
        cur = nxt
        if c + 1 < n_chunks:
            issue(nxt_b, 1 - slot, range((c + 1) * burst, (c + 2) * burst))
    sn = _dot_nt(q, kn)
    t_q = lax.broadcasted_iota(jnp.int32, sn.shape, 0) % n_new
    t_k = lax.broadcasted_iota(jnp.int32, sn.shape, 1)
    parts.append(chunk_part(kn[:, 0:KV_RANK], jnp.where(t_k <= t_q, sn, NEG)))

    m = functools.reduce(jnp.maximum, [pm for pm, _, _ in parts])
    l = jnp.zeros_like(m)
    o = jnp.zeros((q.shape[0], KV_RANK), f32)
    for pm, pl_, po in parts:
        w = jnp.exp2(pm - m)
        l = l + pl_ * w
        o = o + po * w
    o_ref[0] = (o / l).astype(bf16)

    @pl.when(b == n_batch - 1)
    def _():
        wait_all(nxt_b, 1 - slot)


def _attn_sample_call(page_table, q, knew, cache_ckv, cache_krope_t, *, layer):
    n_batch, n_pages = page_table.shape
    past = n_pages * PAGE_SIZE
    rows_q = q.shape[1]
    kern = functools.partial(_attn_sample_kernel, layer=layer, n_batch=n_batch, n_pages=n_pages,
                             n_new=rows_q // N_HEADS, n_chunks=4)
    grid_spec = pltpu.PrefetchScalarGridSpec(
        num_scalar_prefetch=1,
        grid=(n_batch,),
        in_specs=[pl.BlockSpec((1, rows_q, KCAT_W), lambda b, pt: (b, 0, 0)),
                  pl.BlockSpec((1,) + knew.shape[1:], lambda b, pt: (b, 0, 0)),
                  pl.BlockSpec(memory_space=pl.ANY),
                  pl.BlockSpec(memory_space=pl.ANY)],
        out_specs=pl.BlockSpec((1, rows_q, KV_RANK), lambda b, pt: (b, 0, 0)),
        scratch_shapes=[pltpu.VMEM((2, past, KV_RANK), f32),
                        pltpu.VMEM((2, ROPE_DIM, past), f32),
                        pltpu.SemaphoreType.DMA((2, 2))],
    )
    return pl.pallas_call(
        kern,
        grid_spec=grid_spec,
        out_shape=jax.ShapeDtypeStruct((n_batch, rows_q, KV_RANK), bf16),
        compiler_params=pltpu.CompilerParams(dimension_semantics=("arbitrary",), vmem_limit_bytes=VMEM_LIMIT),
        name="attn_sample",
    )(page_table, q, knew, cache_ckv, cache_krope_t)


def _post_kernel(x_ref, o_ref, gatt_ref, mix_ref, wuv_ref, wout_ref, npost_ref, y_ref):
    olat = jnp.concatenate([o_ref[hh] for hh in range(N_HEADS)], axis=1)
    oatt = _dot(olat, wuv_ref[...])
    mixed = jnp.concatenate([(oatt * gatt_ref[...]).astype(bf16), mix_ref[...]], axis=1)
    out = _dot(mixed, wout_ref[...])
    y_ref[...] = x_ref[...] + _rms(out, npost_ref[...])


def _post_call(x, olat, gatt, mix, w, *, tm):
    rows = x.shape[0]
    rowblk = lambda width: pl.BlockSpec((tm, width), lambda i: (i, 0))
    full = lambda a: pl.BlockSpec(a.shape, lambda i: (0,) * a.ndim)
    return pl.pallas_call(
        _post_kernel,
        grid=(rows // tm,),
        in_specs=[rowblk(D_MODEL), pl.BlockSpec((N_HEADS, tm, KV_RANK), lambda i: (0, i, 0)),
                  rowblk(ATT_W), rowblk(POOL_W + CONV_W), full(w['w_uv']), full(w['w_out']), full(w['norm_post'])],
        out_specs=rowblk(D_MODEL),
        out_shape=jax.ShapeDtypeStruct((rows, D_MODEL), f32),
        compiler_params=pltpu.CompilerParams(dimension_semantics=("arbitrary",), vmem_limit_bytes=VMEM_LIMIT),
        name="post",
    )(x, olat, gatt, mix, w['w_uv'], w['w_out'], w['norm_post'])


def _swap_halves(wcols):
    half = wcols.shape[-1] // 2
    return jnp.concatenate([wcols[..., half:], wcols[..., :half]], axis=-1)


def _rope_slot(wcols):
    pad = jnp.zeros((wcols.shape[0], LANES - 2 * ROPE_DIM), wcols.dtype)
    return jnp.concatenate([wcols, _swap_halves(wcols), pad], axis=-1)


def _block_diag(blocks):
    g, a, b = blocks.shape
    eye = jnp.eye(g, dtype=blocks.dtype)
    return jnp.einsum('gab,gh->gahb', blocks, eye).reshape(g * a, g * b)


def _prep_layer(l, norm_pre, w_in, q_norm, w_uq, kv_norm, w_uk, w_uv, pool_w, pool_scale,
                conv_dw, conv_b, conv_ln_g, conv_ln_b, conv_pw, w_out, norm_post):
    wi = w_in[l]
    o_kr = Q_RANK + KV_RANK
    o_pool = o_kr + ROPE_DIM
    win = jnp.concatenate([wi[:, :o_kr], wi[:, o_pool:], _rope_slot(wi[:, o_kr:o_pool])], axis=1)
    wq = w_uq[l].reshape(Q_RANK, N_HEADS, NOPE_DIM + ROPE_DIM)
    wuq = jnp.concatenate(
        [wq[:, :, :NOPE_DIM].reshape(Q_RANK, N_HEADS * NOPE_DIM)]
        + [_rope_slot(wq[:, hh, NOPE_DIM:]) for hh in range(N_HEADS)], axis=1)
    row = lambda v: v.reshape(1, -1)
    return {
        'norm_pre': row(norm_pre[l]), 'w_in': win.astype(bf16), 'q_norm': row(q_norm[l]),
        'w_uq': wuq.astype(bf16), 'kv_norm': row(kv_norm[l]),
        'w_uk': _block_diag(jnp.transpose(w_uk[l], (1, 2, 0))).astype(bf16),
        'w_uv': _block_diag(jnp.transpose(w_uv[l], (1, 0, 2))).astype(bf16),
        'pool_w': _block_diag(pool_w[l]).astype(bf16), 'pool_scale': row(pool_scale[l]),
        'conv_dw': conv_dw[l], 'conv_b': row(conv_b[l]), 'conv_ln_g': row(conv_ln_g[l]),
        'conv_ln_b': row(conv_ln_b[l]), 'conv_pw': conv_pw[l].astype(bf16),
        'w_out': w_out[l].astype(bf16), 'norm_post': row(norm_post[l]),
    }


def _rope_table(pos):
    inv = ROPE_THETA ** (-jnp.arange(0, ROPE_DIM, 2, dtype=f32) / ROPE_DIM)
    ang = pos.astype(f32)[:, None] * inv[None, :]
    cos, sin = jnp.cos(ang), jnp.sin(ang)
    pad = jnp.zeros((pos.shape[0], LANES - 2 * ROPE_DIM), f32)
    return jnp.concatenate([cos, cos, -sin, sin, pad], axis=1)


def kernel(x_prompt, x_sample, cache_ckv, cache_krope, page_table, state_pool, state_conv, norm_pre, w_in,
           q_norm, w_uq, kv_norm, w_uk, w_uv, pool_w, pool_scale, conv_dw, conv_b, conv_ln_g, conv_ln_b,
           conv_pw, w_out, norm_post):
    bp, lp, _ = x_prompt.shape
    bs, ls, _ = x_sample.shape
    assert bp == 1
    depth = w_in.shape[0]
    past = page_table.shape[1] * PAGE_SIZE
    rows_s = bs * ls

    t1_p = _rope_table(jnp.arange(lp, dtype=jnp.int32))
    t1_s = _rope_table(past + jnp.repeat(jnp.arange(ls, dtype=jnp.int32), bs))
    hp = x_prompt.reshape(lp, D_MODEL)
    hs = jnp.transpose(x_sample, (1, 0, 2)).reshape(rows_s, D_MODEL)
    zero_hist = jnp.zeros((32, POOL_W), f32)
    cache_krope_t = jnp.swapaxes(cache_krope, 2, 3)

    def from_time_major(a, width):
        return jnp.transpose(a.reshape(ls, bs, width), (1, 0, 2))

    outs = {k: [] for k in ('ckv_p', 'kr_p', 'pool_p', 'conv_p', 'ckv_s', 'kr_s', 'pool_s', 'conv_s')}
    for l in range(depth):
        w = _prep_layer(l, norm_pre, w_in, q_norm, w_uq, kv_norm, w_uk, w_uv, pool_w, pool_scale,
                        conv_dw, conv_b, conv_ln_g, conv_ln_b, conv_pw, w_out, norm_post)
        q, kcat, ckv, kr, mix, gatt, ptail, ctail = _pre_call(
            hp, t1_p, zero_hist, zero_hist, w, tm=512, stride=1, pos0=0)
        olat = _attn_prompt_call(q, kcat, tq=256)
        hp = _post_call(hp, olat, gatt, mix, w, tm=512)
        outs['ckv_p'].append(ckv.reshape(bp, lp, KV_RANK))
        outs['kr_p'].append(kr.reshape(bp, lp, ROPE_DIM))
        outs['pool_p'].append(ptail[-POOL_HIST:].reshape(bp, POOL_HIST, POOL_W))
        outs['conv_p'].append(ctail[-CONV_HIST:].reshape(bp, CONV_HIST, CONV_W))
        hist_pool = jnp.transpose(state_pool[l], (1, 0, 2)).reshape(POOL_HIST * bs, POOL_W)
        hist_conv = jnp.transpose(state_conv[l], (1, 0, 2)).reshape(CONV_HIST * bs, CONV_W)
        q, kcat, ckv, kr, mix, gatt, ptail, ctail = _pre_call(
            hs, t1_s, hist_pool, hist_conv, w, tm=rows_s, stride=bs, pos0=past)
        q_b = jnp.transpose(q.reshape(N_HEADS, ls, bs, KCAT_W), (2, 0, 1, 3)).reshape(bs, N_HEADS * ls, KCAT_W)
        knew = jnp.pad(from_time_major(kcat, KCAT_W), ((0, 0), (0, 16 - ls), (0, 0)))
        o_b = _attn_sample_call(page_table, q_b, knew, cache_ckv, cache_krope_t, layer=l)
        olat = jnp.transpose(o_b.reshape(bs, N_HEADS, ls, KV_RANK), (1, 2, 0, 3)).reshape(N_HEADS, rows_s, KV_RANK)
        hs = _post_call(hs, olat, gatt, mix, w, tm=rows_s)
        outs['ckv_s'].append(from_time_major(ckv, KV_RANK))
        outs['kr_s'].append(from_time_major(kr, ROPE_DIM))
        outs['pool_s'].append(jnp.concatenate([state_pool[l][:, ls:], from_time_major(ptail, POOL_W)], axis=1))
        outs['conv_s'].append(jnp.concatenate([state_conv[l][:, ls:], from_time_major(ctail, CONV_W)], axis=1))

    y_prompt = hp.reshape(bp, lp, D_MODEL)
    y_sample = from_time_major(hs, D_MODEL)
    return (y_prompt, y_sample,
            jnp.stack(outs['ckv_p']), jnp.stack(outs['kr_p']), jnp.stack(outs['pool_p']), jnp.stack(outs['conv_p']),
            jnp.stack(outs['ckv_s']), jnp.stack(outs['kr_s']), jnp.stack(outs['pool_s']), jnp.stack(outs['conv_s']))
```

```python
import functools
import math

import jax
import jax.numpy as jnp
from jax import lax
from jax.experimental import pallas as pl
from jax.experimental.pallas import tpu as pltpu

D_MODEL = 1024
N_HEADS = 8
NOPE_DIM = 64
ROPE_DIM = 32
V_DIM = 64
Q_RANK = 256
KV_RANK = 128
ATT_W = N_HEADS * V_DIM
POOL_W = 256
POOL_GW = 64
POOL_HIST = 15
CONV_W = 256
CONV_K = 31
CONV_HIST = CONV_K - 1
PAGE_SIZE = 128
ATT_SCALE = (NOPE_DIM + ROPE_DIM) ** -0.5
Q_SCALE = ATT_SCALE * math.log2(math.e)
ROPE_THETA = 10000.0
EPS = 1e-6

LANES = 128
SUBLANES = 8
KCAT_W = 2 * LANES
ONES_COL = KV_RANK + ROPE_DIM
NEG = -1e30

C_Q, C_KV, C_POOL, C_CONV, C_GATE, C_KR, C_END = 0, 256, 384, 640, 1152, 2176, 2304
U_ROPE, U_END = N_HEADS * NOPE_DIM, N_HEADS * NOPE_DIM + N_HEADS * LANES

VMEM_LIMIT = 56 * 1024 * 1024

f32 = jnp.float32
bf16 = jnp.bfloat16


def _dot(a, b):
    return jnp.dot(a, b, preferred_element_type=f32)


def _dot_nt(a, b):
    return lax.dot_general(a, b, (((1,), (1,)), ((), ())), preferred_element_type=f32)


def _rms(x, g):
    return x * lax.rsqrt(jnp.mean(x * x, axis=-1, keepdims=True) + EPS) * g


def _sigmoid(x):
    return 1.0 / (1.0 + jnp.exp(-x))


def _align8(n):
    return (n + 7) // 8 * 8


def _pre_kernel(x_ref, t1_ref, hp_ref, hc_ref, npre_ref, win_ref, qn_ref, wuq_ref, kvn_ref, wuk_ref,
                poolw_ref, pscale_ref, dw_ref, db_ref, lng_ref, lnb_ref, pw_ref,
                q_ref, kcat_ref, ckv_ref, kr_ref, mix_ref, gatt_ref, ptail_ref, ctail_ref,
                pext, cext, buf2, buf4, buf8, ybuf, cshift, *, tm, stride, ph, ch, pos0, tail, carry):
    i = pl.program_id(0)

    @pl.when(i == 0)
    def _():
        pext[0:ph, :] = hp_ref[...]
        cext[0:ch, :] = hc_ref[...]

    h = _rms(x_ref[...], npre_ref[...]).astype(bf16)
    t1 = t1_ref[...]
    lane = lax.broadcasted_iota(jnp.int32, (tm, LANES), 1)
    n = ph + tm

    def rope(slot):
        pr = slot * t1
        return pr + pltpu.roll(pr, LANES - ROPE_DIM, 1)

    uc = _dot(h, win_ref[:, C_CONV:C_GATE])
    cext[ch:ch + tm, :] = uc[:, 0:CONV_W] * _sigmoid(uc[:, CONV_W:2 * CONV_W])
    pext[ph:n, :] = _dot(h, win_ref[:, C_POOL:C_CONV])
    shifted = stride % SUBLANES != 0
    if shifted:
        for j in range(1, SUBLANES):
            cshift[j - 1, SUBLANES:ch + tm, :] = cext[SUBLANES - j:ch + tm - j, :]
    rc = tm // 4

    def conv_rows(r0):
        acc = jnp.broadcast_to(db_ref[...], (rc, CONV_W))
        for k in range(CONV_K):
            back = (CONV_HIST - k) * stride
            j = back % SUBLANES if shifted else 0
            off = ch + r0 - (back - j)
            rows = cext[off:off + rc, :] if j == 0 else cshift[j - 1, off:off + rc, :]
            acc = acc + rows * dw_ref[k:k + 1, :]
        ybuf[r0:r0 + rc, :] = acc

    conv_rows(0)
    sgate = _dot(h, win_ref[:, C_GATE:C_KR])
    sgate = sgate * _sigmoid(sgate)
    gatt_ref[...] = sgate[:, 0:ATT_W]

    conv_rows(rc)
    cq = _rms(_dot(h, win_ref[:, C_Q:C_KV]), qn_ref[...]).astype(bf16)
    qz = _dot(cq, wuq_ref[...])
    qabs = _dot(qz[:, 0:U_ROPE].astype(bf16), wuk_ref[...])
    for hh in range(N_HEADS):
        q_ref[hh, :, 0:LANES] = (qabs[:, hh * LANES:(hh + 1) * LANES] * Q_SCALE).astype(bf16)
        qr = rope(qz[:, U_ROPE + hh * LANES:U_ROPE + (hh + 1) * LANES])
        q_ref[hh, :, LANES:KCAT_W] = jnp.where(lane < ROPE_DIM, qr * Q_SCALE, 0.0).astype(bf16)

    conv_rows(2 * rc)
    ckv =_rms(_dot(h, win_ref[:, C_KV:C_POOL]), kvn_ref[...])
    ckv_ref[...] = ckv
    kr = rope(_dot(h, win_ref[:, C_KR:C_END]))
    kr_ref[...] = kr[:, 0:ROPE_DIM]
    kcat_ref[:, 0:LANES] = ckv.astype(bf16)
    kcat_ref[:, LANES:KCAT_W] = jnp.where(
        lane < ROPE_DIM, kr, jnp.where(lane == ROPE_DIM, 1.0, 0.0)).astype(bf16)

    conv_rows(3 * rc)

    lo2 = _align8(stride)
    lo4 = _align8(lo2 + 2 * stride)
    lo8 = _align8(lo4 + 4 * stride)
    buf2[lo2:n, :] = pext[lo2:n, :] + pext[lo2 - stride:n - stride, :]
    buf4[lo4:n, :] = buf2[lo4:n, :] + buf2[lo4 - 2 * stride:n - 2 * stride, :]
    buf8[lo8:n, :] = buf4[lo8:n, :] + buf4[lo8 - 4 * stride:n - 4 * stride, :]
    s16 = buf8[ph:n, :] + buf8[ph - 8 * stride:n - 8 * stride, :]
    col = lax.broadcasted_iota(jnp.int32, (tm, POOL_W), 1)
    row = lax.broadcasted_iota(jnp.int32, (tm, POOL_W), 0)
    wsum = jnp.where(col < POOL_GW, buf2[ph:n, :],
                     jnp.where(col < 2 * POOL_GW, buf4[ph:n, :],
                               jnp.where(col < 3 * POOL_GW, buf8[ph:n, :], s16)))
    win = jnp.where(col < POOL_GW, 2, jnp.where(col < 2 * POOL_GW, 4, jnp.where(col < 3 * POOL_GW, 8, 16)))
    pos = pos0 + (i * tm + row) // stride
    cnt = jnp.minimum(pos + 1, win).astype(f32)
    d = wsum / cnt - pext[ph:n, :]
    ypool = _dot(d.astype(bf16), poolw_ref[...]) * pscale_ref[...]
    mix_ref[:, 0:POOL_W] = (ypool * sgate[:, ATT_W:ATT_W + POOL_W]).astype(bf16)

    y = ybuf[...]
    mu = jnp.mean(y, axis=-1, keepdims=True)
    yc = y - mu
    var = jnp.mean(yc * yc, axis=-1, keepdims=True)
    yn = yc * lax.rsqrt(var + EPS) * lng_ref[...] + lnb_ref[...]
    yconv = _dot((yn * _sigmoid(yn)).astype(bf16), pw_ref[...])
    mix_ref[:, POOL_W:POOL_W + CONV_W] = (yconv * sgate[:, ATT_W + POOL_W:]).astype(bf16)

    ptail_ref[...] = pext[n - tail:n, :]
    ctail_ref[...] = cext[ch + tm - tail:ch + tm, :]
    if carry:
        pext[0:ph, :] = pext[tm:tm + ph, :]
        cext[0:ch, :] = cext[tm:tm + ch, :]


def _pre_call(x, t1, hist_pool, hist_conv, w, *, tm, stride, pos0):
    rows = x.shape[0]
    ph, ch = hist_pool.shape[0], hist_conv.shape[0]
    nt = rows // tm
    tail = min(tm, _align8(CONV_HIST * stride))
    kern = functools.partial(_pre_kernel, tm=tm, stride=stride, ph=ph, ch=ch, pos0=pos0, tail=tail,
                             carry=nt > 1)
    rowblk = lambda width: pl.BlockSpec((tm, width), lambda i: (i, 0))
    full = lambda a: pl.BlockSpec(a.shape, lambda i: (0,) * a.ndim)
    weights = (w['norm_pre'], w['w_in'], w['q_norm'], w['w_uq'], w['kv_norm'], w['w_uk'], w['pool_w'],
               w['pool_scale'], w['conv_dw'], w['conv_b'], w['conv_ln_g'], w['conv_ln_b'], w['conv_pw'])
    return pl.pallas_call(
        kern,
        grid=(nt,),
        in_specs=[rowblk(D_MODEL), rowblk(LANES), full(hist_pool), full(hist_conv)] + [full(a) for a in weights],
        out_specs=[
            pl.BlockSpec((N_HEADS, tm, KCAT_W), lambda i: (0, i, 0)),
            rowblk(KCAT_W), rowblk(KV_RANK), rowblk(ROPE_DIM), rowblk(POOL_W + CONV_W), rowblk(ATT_W),
            pl.BlockSpec((tail, POOL_W), lambda i: (0, 0)),
            pl.BlockSpec((tail, CONV_W), lambda i: (0, 0)),
        ],
        out_shape=[
            jax.ShapeDtypeStruct((N_HEADS, rows, KCAT_W), bf16),
            jax.ShapeDtypeStruct((rows, KCAT_W), bf16),
            jax.ShapeDtypeStruct((rows, KV_RANK), f32),
            jax.ShapeDtypeStruct((rows, ROPE_DIM), f32),
            jax.ShapeDtypeStruct((rows, POOL_W + CONV_W), bf16),
            jax.ShapeDtypeStruct((rows, ATT_W), f32),
            jax.ShapeDtypeStruct((tail, POOL_W), f32),
            jax.ShapeDtypeStruct((tail, CONV_W), f32),
        ],
        scratch_shapes=[
            pltpu.VMEM((ph + tm, POOL_W), f32), pltpu.VMEM((ch + tm, CONV_W), f32),
            pltpu.VMEM((ph + tm, POOL_W), f32), pltpu.VMEM((ph + tm, POOL_W), f32),
            pltpu.VMEM((ph + tm, POOL_W), f32), pltpu.VMEM((tm, CONV_W), f32),
            pltpu.VMEM((SUBLANES - 1, ch + tm if stride % SUBLANES else SUBLANES, CONV_W), f32),
        ],
        compiler_params=pltpu.CompilerParams(dimension_semantics=("arbitrary",), vmem_limit_bytes=VMEM_LIMIT),
        name="pre",
    )(x, t1, hist_pool, hist_conv, *weights)


def _attn_prompt_kernel(q_ref, qnext_ref, k_ref, o_ref, m_scr, acc_scr, s_a, s_b, *, tq):
    g = pl.program_id(0)
    tk = 2 * tq
    m_rows = N_HEADS * tq

    def key_tile(t):
        return k_ref[pl.ds(pl.multiple_of(t * tk, tk), tk), :]

    def scores(q, t, dst):
        dst[...] = _dot_nt(q.reshape(m_rows, KCAT_W), key_tile(t))

    def q_sub(sub):
        return q_ref[:, sub * tq:(sub + 1) * tq, :]

    def consume(src, sub, t, masked):
        s = src[...]
        if masked:
            qoff = sub * tq + (lax.broadcasted_iota(jnp.int32, (m_rows, tk), 0) & (tq - 1))
            s = jnp.where(lax.broadcasted_iota(jnp.int32, (m_rows, tk), 1) <= qoff, s, NEG)
        m_prev = m_scr[sub]
        m_new = jnp.maximum(m_prev, jnp.max(s, axis=1, keepdims=True))
        alpha = jnp.exp2(m_prev - m_new)
        p = jnp.exp2(s - jnp.concatenate([m_new] * (tk // LANES), axis=1))
        pv = _dot(p.astype(bf16), key_tile(t))
        acc_scr[sub] = acc_scr[sub] * jnp.concatenate([alpha] * (KCAT_W // LANES), axis=1) + pv
        m_scr[sub] = m_new

    def reset():
        m_scr[...] = jnp.full(m_scr.shape, NEG, f32)
        acc_scr[...] = jnp.zeros(acc_scr.shape, f32)

    @pl.when(g == 0)
    def _():
        reset()
        scores(q_sub(0), 0, s_a)

    def pair(t, c):
        scores(q_sub(1), t, s_b)
        consume(s_a, 0, t, False)
        scores(q_sub(0), t + 1, s_a)
        consume(s_b, 1, t, False)
        return c

    def quad(j, c):
        pair(2 * j, c)
        return pair(2 * j + 1, c)

    lax.fori_loop(0, g // 2, quad, 0)
    lax.fori_loop(2 * (g // 2), g, pair, 0)

    scores(q_sub(1), g, s_b)
    consume(s_a, 0, g, True)
    scores(qnext_ref[...], 0, s_a)
    consume(s_b, 1, g, True)
    for sub in range(2):
        acc = acc_scr[sub]
        out = acc[:, 0:KV_RANK] / acc[:, ONES_COL:ONES_COL + 1]
        o_ref[:, sub * tq:(sub + 1) * tq, :] = out.astype(bf16).reshape(N_HEADS, tq, KV_RANK)
    reset()


def _attn_prompt_call(q, kcat, *, tq):
    rows = kcat.shape[0]
    tk = 2 * tq
    nsub = rows // tq
    kern = functools.partial(_attn_prompt_kernel, tq=tq)
    return pl.pallas_call(
        kern,
        grid=(rows // tk,),
        in_specs=[pl.BlockSpec((N_HEADS, tk, KCAT_W), lambda g: (0, g, 0)),
                  pl.BlockSpec((N_HEADS, tq, KCAT_W), lambda g: (0, jnp.minimum(2 * g + 2, nsub - 1), 0)),
                  pl.BlockSpec((rows, KCAT_W), lambda g: (0, 0))],
        out_specs=pl.BlockSpec((N_HEADS, tk, KV_RANK), lambda g: (0, g, 0)),
        out_shape=jax.ShapeDtypeStruct((N_HEADS, rows, KV_RANK), bf16),
        scratch_shapes=[pltpu.VMEM((2, N_HEADS * tq, LANES), f32), pltpu.VMEM((2, N_HEADS * tq, KCAT_W), f32),
                        pltpu.VMEM((N_HEADS * tq, tk), f32), pltpu.VMEM((N_HEADS * tq, tk), f32)],
        compiler_params=pltpu.CompilerParams(dimension_semantics=("arbitrary",), vmem_limit_bytes=VMEM_LIMIT),
        name="attn_prompt",
    )(q, q, kcat)


def _attn_sample_kernel(pt_ref, q_ref, knew_ref, ckv_hbm, krt_hbm, o_ref, kbuf, rbuf, sem,
                        *, layer, n_batch, n_pages, n_new, n_chunks):
    b = pl.program_id(0)
    slot = b % 2

    def page_copies(bb, sl, p):
        page = pt_ref[bb, p]
        dst = pl.ds(p * PAGE_SIZE, PAGE_SIZE)
        return (pltpu.make_async_copy(ckv_hbm.at[layer, page], kbuf.at[sl, dst], sem.at[0, sl]),
                pltpu.make_async_copy(krt_hbm.at[layer, page], rbuf.at[sl, :, dst], sem.at[1, sl]))

    def issue(bb, sl, pages):
        for p in pages:
            for cpy in page_copies(bb, sl, p):
                cpy.start()

    def wait_all(bb, sl):
        for p in range(n_pages):
            for cpy in page_copies(bb, sl, p):
                cpy.wait()

    @pl.when(b == 0)
    def _():
        issue(0, 0, range(n_pages))

    @pl.when(b + 1 < n_batch)
    def _():
        issue(b + 1, 1 - slot, range(n_pages))

    wait_all(b, slot)

    q = q_ref[0]
    q_lat, q_rot = q[:, 0:KV_RANK], q[:, KV_RANK:KV_RANK + ROPE_DIM]
    kn = knew_ref[0]
    ck = n_pages * PAGE_SIZE // n_chunks

    def chunk_scores(c):
        kb = kbuf[slot, c * ck:(c + 1) * ck, :].astype(bf16)
        rb = rbuf[slot, :, c * ck:(c + 1) * ck].astype(bf16)
        return kb, _dot_nt(q_lat, kb) + _dot(q_rot, rb)

    def chunk_part(v, s):
        m = jnp.max(s, axis=1, keepdims=True)
        p = jnp.exp2(s - m)
        return m, jnp.sum(p, axis=1, keepdims=True), _dot(p.astype(bf16), v)

    parts = []
    cur = chunk_scores(0)
    for c in range(n_chunks):
        nxt = chunk_scores(c + 1) if c + 1 < n_chunks else None
        parts.append(chunk_part(*cur))
        cur = nxt
    sn = _dot_nt(q, kn)
    t_q = lax.broadcasted_iota(jnp.int32, sn.shape, 0) % n_new
    t_k = lax.broadcasted_iota(jnp.int32, sn.shape, 1)
    parts.append(chunk_part(kn[:, 0:KV_RANK], jnp.where(t_k <= t_q, sn, NEG)))

    m = functools.reduce(jnp.maximum, [pm for pm, _, _ in parts])
    l = jnp.zeros_like(m)
    o = jnp.zeros((q.shape[0], KV_RANK), f32)
    for pm, pl_, po in parts:
        w = jnp.exp2(pm - m)
        l = l + pl_ * w
        o = o + po * w
    o_ref[0] = (o / l).astype(bf16)


def _attn_sample_call(page_table, q, knew, cache_ckv, cache_krope_t, *, layer):
    n_batch, n_pages = page_table.shape
    past = n_pages * PAGE_SIZE
    rows_q = q.shape[1]
    kern = functools.partial(_attn_sample_kernel, layer=layer, n_batch=n_batch, n_pages=n_pages,
                             n_new=rows_q // N_HEADS, n_chunks=4)
    grid_spec = pltpu.PrefetchScalarGridSpec(
        num_scalar_prefetch=1,
        grid=(n_batch,),
        in_specs=[pl.BlockSpec((1, rows_q, KCAT_W), lambda b, pt: (b, 0, 0)),
                  pl.BlockSpec((1,) + knew.shape[1:], lambda b, pt: (b, 0, 0)),
                  pl.BlockSpec(memory_space=pl.ANY),
                  pl.BlockSpec(memory_space=pl.ANY)],
        out_specs=pl.BlockSpec((1, rows_q, KV_RANK), lambda b, pt: (b, 0, 0)),
        scratch_shapes=[pltpu.VMEM((2, past, KV_RANK), f32),
                        pltpu.VMEM((2, ROPE_DIM, past), f32),
                        pltpu.SemaphoreType.DMA((2, 2))],
    )
    return pl.pallas_call(
        kern,
        grid_spec=grid_spec,
        out_shape=jax.ShapeDtypeStruct((n_batch, rows_q, KV_RANK), bf16),
        compiler_params=pltpu.CompilerParams(dimension_semantics=("arbitrary",), vmem_limit_bytes=VMEM_LIMIT),
        name="attn_sample",
    )(page_table, q, knew, cache_ckv, cache_krope_t)


def _post_kernel(x_ref, o_ref, gatt_ref, mix_ref, wuv_ref, wout_ref, npost_ref, y_ref):
    olat = jnp.concatenate([o_ref[hh] for hh in range(N_HEADS)], axis=1)
    oatt = _dot(olat, wuv_ref[...])
    mixed = jnp.concatenate([(oatt * gatt_ref[...]).astype(bf16), mix_ref[...]], axis=1)
    out = _dot(mixed, wout_ref[...])
    y_ref[...] = x_ref[...] + _rms(out, npost_ref[...])


def _post_call(x, olat, gatt, mix, w, *, tm):
    rows = x.shape[0]
    rowblk = lambda width: pl.BlockSpec((tm, width), lambda i: (i, 0))
    full = lambda a: pl.BlockSpec(a.shape, lambda i: (0,) * a.ndim)
    return pl.pallas_call(
        _post_kernel,
        grid=(rows // tm,),
        in_specs=[rowblk(D_MODEL), pl.BlockSpec((N_HEADS, tm, KV_RANK), lambda i: (0, i, 0)),
                  rowblk(ATT_W), rowblk(POOL_W + CONV_W), full(w['w_uv']), full(w['w_out']), full(w['norm_post'])],
        out_specs=rowblk(D_MODEL),
        out_shape=jax.ShapeDtypeStruct((rows, D_MODEL), f32),
        compiler_params=pltpu.CompilerParams(dimension_semantics=("arbitrary",), vmem_limit_bytes=VMEM_LIMIT),
        name="post",
    )(x, olat, gatt, mix, w['w_uv'], w['w_out'], w['norm_post'])


def _swap_halves(wcols):
    half = wcols.shape[-1] // 2
    return jnp.concatenate([wcols[..., half:], wcols[..., :half]], axis=-1)


def _rope_slot(wcols):
    pad = jnp.zeros((wcols.shape[0], LANES - 2 * ROPE_DIM), wcols.dtype)
    return jnp.concatenate([wcols, _swap_halves(wcols), pad], axis=-1)


def _block_diag(blocks):
    g, a, b = blocks.shape
    eye = jnp.eye(g, dtype=blocks.dtype)
    return jnp.einsum('gab,gh->gahb', blocks, eye).reshape(g * a, g * b)


def _prep_layer(l, norm_pre, w_in, q_norm, w_uq, kv_norm, w_uk, w_uv, pool_w, pool_scale,
                conv_dw, conv_b, conv_ln_g, conv_ln_b, conv_pw, w_out, norm_post):
    wi = w_in[l]
    o_kr = Q_RANK + KV_RANK
    o_pool = o_kr + ROPE_DIM
    win = jnp.concatenate([wi[:, :o_kr], wi[:, o_pool:], _rope_slot(wi[:, o_kr:o_pool])], axis=1)
    wq = w_uq[l].reshape(Q_RANK, N_HEADS, NOPE_DIM + ROPE_DIM)
    wuq = jnp.concatenate(
        [wq[:, :, :NOPE_DIM].reshape(Q_RANK, N_HEADS * NOPE_DIM)]
        + [_rope_slot(wq[:, hh, NOPE_DIM:]) for hh in range(N_HEADS)], axis=1)
    row = lambda v: v.reshape(1, -1)
    return {
        'norm_pre': row(norm_pre[l]), 'w_in': win.astype(bf16), 'q_norm': row(q_norm[l]),
        'w_uq': wuq.astype(bf16), 'kv_norm': row(kv_norm[l]),
        'w_uk': _block_diag(jnp.transpose(w_uk[l], (1, 2, 0))).astype(bf16),
        'w_uv': _block_diag(jnp.transpose(w_uv[l], (1, 0, 2))).astype(bf16),
        'pool_w': _block_diag(pool_w[l]).astype(bf16), 'pool_scale': row(pool_scale[l]),
        'conv_dw': conv_dw[l], 'conv_b': row(conv_b[l]), 'conv_ln_g': row(conv_ln_g[l]),
        'conv_ln_b': row(conv_ln_b[l]), 'conv_pw': conv_pw[l].astype(bf16),
        'w_out': w_out[l].astype(bf16), 'norm_post': row(norm_post[l]),
    }


def _rope_table(pos):
    inv = ROPE_THETA ** (-jnp.arange(0, ROPE_DIM, 2, dtype=f32) / ROPE_DIM)
    ang = pos.astype(f32)[:, None] * inv[None, :]
    cos, sin = jnp.cos(ang), jnp.sin(ang)
    pad = jnp.zeros((pos.shape[0], LANES - 2 * ROPE_DIM), f32)
    return jnp.concatenate([cos, cos, -sin, sin, pad], axis=1)


def kernel(x_prompt, x_sample, cache_ckv, cache_krope, page_table, state_pool, state_conv, norm_pre, w_in,
           q_norm, w_uq, kv_norm, w_uk, w_uv, pool_w, pool_scale, conv_dw, conv_b, conv_ln_g, conv_ln_b,
           conv_pw, w_out, norm_post):
    bp, lp, _ = x_prompt.shape
    bs, ls, _ = x_sample.shape
    assert bp == 1
    depth = w_in.shape[0]
    past = page_table.shape[1] * PAGE_SIZE
    rows_s = bs * ls

    t1_p = _rope_table(jnp.arange(lp, dtype=jnp.int32))
    t1_s = _rope_table(past + jnp.repeat(jnp.arange(ls, dtype=jnp.int32), bs))
    hp = x_prompt.reshape(lp, D_MODEL)
    hs = jnp.transpose(x_sample, (1, 0, 2)).reshape(rows_s, D_MODEL)
    zero_hist = jnp.zeros((32, POOL_W), f32)
    cache_krope_t = jnp.swapaxes(cache_krope, 2, 3)

    def from_time_major(a, width):
        return jnp.transpose(a.reshape(ls, bs, width), (1, 0, 2))

    outs = {k: [] for k in ('ckv_p', 'kr_p', 'pool_p', 'conv_p', 'ckv_s', 'kr_s', 'pool_s', 'conv_s')}
    for l in range(depth):
        w = _prep_layer(l, norm_pre, w_in, q_norm, w_uq, kv_norm, w_uk, w_uv, pool_w, pool_scale,
                        conv_dw, conv_b, conv_ln_g, conv_ln_b, conv_pw, w_out, norm_post)
        q, kcat, ckv, kr, mix, gatt, ptail, ctail = _pre_call(
            hp, t1_p, zero_hist, zero_hist, w, tm=512, stride=1, pos0=0)
        olat = _attn_prompt_call(q, kcat, tq=256)
        hp = _post_call(hp, olat, gatt, mix, w, tm=512)
        outs['ckv_p'].append(ckv.reshape(bp, lp, KV_RANK))
        outs['kr_p'].append(kr.reshape(bp, lp, ROPE_DIM))
        outs['pool_p'].append(ptail[-POOL_HIST:].reshape(bp, POOL_HIST, POOL_W))
        outs['conv_p'].append(ctail[-CONV_HIST:].reshape(bp, CONV_HIST, CONV_W))
        hist_pool = jnp.transpose(state_pool[l], (1, 0, 2)).reshape(POOL_HIST * bs, POOL_W)
        hist_conv = jnp.transpose(state_conv[l], (1, 0, 2)).reshape(CONV_HIST * bs, CONV_W)
        q, kcat, ckv, kr, mix, gatt, ptail, ctail = _pre_call(
            hs, t1_s, hist_pool, hist_conv, w, tm=rows_s, stride=bs, pos0=past)
        q_b = jnp.transpose(q.reshape(N_HEADS, ls, bs, KCAT_W), (2, 0, 1, 3)).reshape(bs, N_HEADS * ls, KCAT_W)
        knew = jnp.pad(from_time_major(kcat, KCAT_W), ((0, 0), (0, 16 - ls), (0, 0)))
        o_b = _attn_sample_call(page_table, q_b, knew, cache_ckv, cache_krope_t, layer=l)
        olat = jnp.transpose(o_b.reshape(bs, N_HEADS, ls, KV_RANK), (1, 2, 0, 3)).reshape(N_HEADS, rows_s, KV_RANK)
        hs = _post_call(hs, olat, gatt, mix, w, tm=rows_s)
        outs['ckv_s'].append(from_time_major(ckv, KV_RANK))
        outs['kr_s'].append(from_time_major(kr, ROPE_DIM))
        outs['pool_s'].append(jnp.concatenate([state_pool[l][:, ls:], from_time_major(ptail, POOL_W)], axis=1))
        outs['conv_s'].append(jnp.concatenate([state_conv[l][:, ls:], from_time_major(ctail, CONV_W)], axis=1))

    y_prompt = hp.reshape(bp, lp, D_MODEL)
    y_sample = from_time_major(hs, D_MODEL)
    return (y_prompt, y_sample,
            jnp.stack(outs['ckv_p']), jnp.stack(outs['kr_p']), jnp.stack(outs['pool_p']), jnp.stack(outs['conv_p']),
            jnp.stack(outs['ckv_s']), jnp.stack(outs['kr_s']), jnp.stack(outs['pool_s']), jnp.stack(outs['conv_s']))
```

```python
import functools
import math

import jax
import jax.numpy as jnp
from jax import lax
from jax.experimental import pallas as pl
from jax.experimental.pallas import tpu as pltpu

D_MODEL = 1024
N_HEADS = 8
NOPE_DIM = 64
ROPE_DIM = 32
V_DIM = 64
Q_RANK = 256
KV_RANK = 128
ATT_W = N_HEADS * V_DIM
POOL_W = 256
POOL_GW = 64
POOL_HIST = 15
CONV_W = 256
CONV_K = 31
CONV_HIST = CONV_K - 1
PAGE_SIZE = 128
ATT_SCALE = (NOPE_DIM + ROPE_DIM) ** -0.5
Q_SCALE = ATT_SCALE * math.log2(math.e)
ROPE_THETA = 10000.0
EPS = 1e-6

LANES = 128
SUBLANES = 8
KCAT_W = 2 * LANES
ONES_COL = KV_RANK + ROPE_DIM
NEG = -1e30

C_Q, C_KV, C_POOL, C_CONV, C_GATE, C_KR, C_END = 0, 256, 384, 640, 1152, 2176, 2304
U_ROPE, U_END = N_HEADS * NOPE_DIM, N_HEADS * NOPE_DIM + N_HEADS * LANES

VMEM_LIMIT = 56 * 1024 * 1024

TM_PROMPT = 512
TQ_PROMPT = 256
SAMPLE_CHUNKS = 4

f32 = jnp.float32
bf16 = jnp.bfloat16


def _dot(a, b):
    return jnp.dot(a, b, preferred_element_type=f32)


def _dot_nt(a, b):
    return lax.dot_general(a, b, (((1,), (1,)), ((), ())), preferred_element_type=f32)


def _rms(x, g):
    return x * lax.rsqrt(jnp.mean(x * x, axis=-1, keepdims=True) + EPS) * g


def _sigmoid(x):
    return 1.0 / (1.0 + jnp.exp(-x))


def _align8(n):
    return (n + 7) // 8 * 8


def _pre_kernel(x_ref, t1_ref, hp_ref, hc_ref, npre_ref, win_ref, qn_ref, wuq_ref, kvn_ref, wuk_ref,
                poolw_ref, pscale_ref, dw_ref, db_ref, lng_ref, lnb_ref, pw_ref,
                q_ref, kcat_ref, ckv_ref, kr_ref, mix_ref, gatt_ref, ptail_ref, ctail_ref,
                pext, cext, buf2, buf4, buf8, ybuf, cshift, *, tm, stride, ph, ch, pos0, tail, carry):
    i = pl.program_id(0)

    @pl.when(i == 0)
    def _():
        pext[0:ph, :] = hp_ref[...]
        cext[0:ch, :] = hc_ref[...]

    h = _rms(x_ref[...], npre_ref[...]).astype(bf16)
    t1 = t1_ref[...]
    lane = lax.broadcasted_iota(jnp.int32, (tm, LANES), 1)
    n = ph + tm

    def rope(slot):
        pr = slot * t1
        return pr + pltpu.roll(pr, LANES - ROPE_DIM, 1)

    uc = _dot(h, win_ref[:, C_CONV:C_GATE])
    cext[ch:ch + tm, :] = uc[:, 0:CONV_W] * _sigmoid(uc[:, CONV_W:2 * CONV_W])
    pext[ph:n, :] = _dot(h, win_ref[:, C_POOL:C_CONV])
    shifted = stride % SUBLANES != 0
    if shifted:
        for j in range(1, SUBLANES):
            cshift[j - 1, SUBLANES:ch + tm, :] = cext[SUBLANES - j:ch + tm - j, :]
    rc = tm // 4

    def conv_rows(r0):
        acc = jnp.broadcast_to(db_ref[...], (rc, CONV_W))
        for k in range(CONV_K):
            back = (CONV_HIST - k) * stride
            j = back % SUBLANES if shifted else 0
            off = ch + r0 - (back - j)
            rows = cext[off:off + rc, :] if j == 0 else cshift[j - 1, off:off + rc, :]
            acc = acc + rows * dw_ref[k:k + 1, :]
        ybuf[r0:r0 + rc, :] = acc

    conv_rows(0)
    sgate = _dot(h, win_ref[:, C_GATE:C_KR])
    sgate = sgate * _sigmoid(sgate)
    gatt_ref[...] = sgate[:, 0:ATT_W]

    conv_rows(rc)
    cq = _rms(_dot(h, win_ref[:, C_Q:C_KV]), qn_ref[...]).astype(bf16)
    qz = _dot(cq, wuq_ref[...])
    qabs = _dot(qz[:, 0:U_ROPE].astype(bf16), wuk_ref[...])
    for hh in range(N_HEADS):
        q_ref[hh, :, 0:LANES] = (qabs[:, hh * LANES:(hh + 1) * LANES] * Q_SCALE).astype(bf16)
        qr = rope(qz[:, U_ROPE + hh * LANES:U_ROPE + (hh + 1) * LANES])
        q_ref[hh, :, LANES:KCAT_W] = jnp.where(lane < ROPE_DIM, qr * Q_SCALE, 0.0).astype(bf16)

    conv_rows(2 * rc)
    ckv =_rms(_dot(h, win_ref[:, C_KV:C_POOL]), kvn_ref[...])
    ckv_ref[...] = ckv
    kr = rope(_dot(h, win_ref[:, C_KR:C_END]))
    kr_ref[...] = kr[:, 0:ROPE_DIM]
    kcat_ref[:, 0:LANES] = ckv.astype(bf16)
    kcat_ref[:, LANES:KCAT_W] = jnp.where(
        lane < ROPE_DIM, kr, jnp.where(lane == ROPE_DIM, 1.0, 0.0)).astype(bf16)

    conv_rows(3 * rc)

    lo2 = _align8(stride)
    lo4 = _align8(lo2 + 2 * stride)
    lo8 = _align8(lo4 + 4 * stride)
    buf2[lo2:n, :] = pext[lo2:n, :] + pext[lo2 - stride:n - stride, :]
    buf4[lo4:n, :] = buf2[lo4:n, :] + buf2[lo4 - 2 * stride:n - 2 * stride, :]
    buf8[lo8:n, :] = buf4[lo8:n, :] + buf4[lo8 - 4 * stride:n - 4 * stride, :]
    s16 = buf8[ph:n, :] + buf8[ph - 8 * stride:n - 8 * stride, :]
    col = lax.broadcasted_iota(jnp.int32, (tm, POOL_W), 1)
    row = lax.broadcasted_iota(jnp.int32, (tm, POOL_W), 0)
    wsum = jnp.where(col < POOL_GW, buf2[ph:n, :],
                     jnp.where(col < 2 * POOL_GW, buf4[ph:n, :],
                               jnp.where(col < 3 * POOL_GW, buf8[ph:n, :], s16)))
    win = jnp.where(col < POOL_GW, 2, jnp.where(col < 2 * POOL_GW, 4, jnp.where(col < 3 * POOL_GW, 8, 16)))
    pos = pos0 + (i * tm + row) // stride
    cnt = jnp.minimum(pos + 1, win).astype(f32)
    d = wsum / cnt - pext[ph:n, :]
    ypool = _dot(d.astype(bf16), poolw_ref[...]) * pscale_ref[...]
    mix_ref[:, 0:POOL_W] = (ypool * sgate[:, ATT_W:ATT_W + POOL_W]).astype(bf16)

    y = ybuf[...]
    mu = jnp.mean(y, axis=-1, keepdims=True)
    yc = y - mu
    var = jnp.mean(yc * yc, axis=-1, keepdims=True)
    yn = yc * lax.rsqrt(var + EPS) * lng_ref[...] + lnb_ref[...]
    yconv = _dot((yn * _sigmoid(yn)).astype(bf16), pw_ref[...])
    mix_ref[:, POOL_W:POOL_W + CONV_W] = (yconv * sgate[:, ATT_W + POOL_W:]).astype(bf16)

    ptail_ref[...] = pext[n - tail:n, :]
    ctail_ref[...] = cext[ch + tm - tail:ch + tm, :]
    if carry:
        pext[0:ph, :] = pext[tm:tm + ph, :]
        cext[0:ch, :] = cext[tm:tm + ch, :]


def _pre_call(x, t1, hist_pool, hist_conv, w, *, tm, stride, pos0):
    rows = x.shape[0]
    ph, ch = hist_pool.shape[0], hist_conv.shape[0]
    nt = rows // tm
    tail = min(tm, _align8(CONV_HIST * stride))
    kern = functools.partial(_pre_kernel, tm=tm, stride=stride, ph=ph, ch=ch, pos0=pos0, tail=tail,
                             carry=nt > 1)
    rowblk = lambda width: pl.BlockSpec((tm, width), lambda i: (i, 0))
    full = lambda a: pl.BlockSpec(a.shape, lambda i: (0,) * a.ndim)
    weights = (w['norm_pre'], w['w_in'], w['q_norm'], w['w_uq'], w['kv_norm'], w['w_uk'], w['pool_w'],
               w['pool_scale'], w['conv_dw'], w['conv_b'], w['conv_ln_g'], w['conv_ln_b'], w['conv_pw'])
    return pl.pallas_call(
        kern,
        grid=(nt,),
        in_specs=[rowblk(D_MODEL), rowblk(LANES), full(hist_pool), full(hist_conv)] + [full(a) for a in weights],
        out_specs=[
            pl.BlockSpec((N_HEADS, tm, KCAT_W), lambda i: (0, i, 0)),
            rowblk(KCAT_W), rowblk(KV_RANK), rowblk(ROPE_DIM), rowblk(POOL_W + CONV_W), rowblk(ATT_W),
            pl.BlockSpec((tail, POOL_W), lambda i: (0, 0)),
            pl.BlockSpec((tail, CONV_W), lambda i: (0, 0)),
        ],
        out_shape=[
            jax.ShapeDtypeStruct((N_HEADS, rows, KCAT_W), bf16),
            jax.ShapeDtypeStruct((rows, KCAT_W), bf16),
            jax.ShapeDtypeStruct((rows, KV_RANK), f32),
            jax.ShapeDtypeStruct((rows, ROPE_DIM), f32),
            jax.ShapeDtypeStruct((rows, POOL_W + CONV_W), bf16),
            jax.ShapeDtypeStruct((rows, ATT_W), f32),
            jax.ShapeDtypeStruct((tail, POOL_W), f32),
            jax.ShapeDtypeStruct((tail, CONV_W), f32),
        ],
        scratch_shapes=[
            pltpu.VMEM((ph + tm, POOL_W), f32), pltpu.VMEM((ch + tm, CONV_W), f32),
            pltpu.VMEM((ph + tm, POOL_W), f32), pltpu.VMEM((ph + tm, POOL_W), f32),
            pltpu.VMEM((ph + tm, POOL_W), f32), pltpu.VMEM((tm, CONV_W), f32),
            pltpu.VMEM((SUBLANES - 1, ch + tm if stride % SUBLANES else SUBLANES, CONV_W), f32),
        ],
        compiler_params=pltpu.CompilerParams(dimension_semantics=("arbitrary",), vmem_limit_bytes=VMEM_LIMIT),
        name="pre",
    )(x, t1, hist_pool, hist_conv, *weights)


def _attn_prompt_kernel(q_ref, qnext_ref, k_ref, o_ref, m_scr, acc_scr, s_a, s_b, *, tq):
    g = pl.program_id(0)
    tk = 2 * tq
    m_rows = N_HEADS * tq

    def key_tile(t):
        return k_ref[pl.ds(pl.multiple_of(t * tk, tk), tk), :]

    def scores(q, t, dst):
        dst[...] = _dot_nt(q.reshape(m_rows, KCAT_W), key_tile(t))

    def q_sub(sub):
        return q_ref[:, sub * tq:(sub + 1) * tq, :]

    def consume(src, sub, t, masked):
        s = src[...]
        if masked:
            qoff = sub * tq + (lax.broadcasted_iota(jnp.int32, (m_rows, tk), 0) & (tq - 1))
            s = jnp.where(lax.broadcasted_iota(jnp.int32, (m_rows, tk), 1) <= qoff, s, NEG)
        m_prev = m_scr[sub]
        m_new = jnp.maximum(m_prev, jnp.max(s, axis=1, keepdims=True))
        alpha = jnp.exp2(m_prev - m_new)
        p = jnp.exp2(s - jnp.concatenate([m_new] * (tk // LANES), axis=1))
        pv = _dot(p.astype(bf16), key_tile(t))
        acc_scr[sub] = acc_scr[sub] * jnp.concatenate([alpha] * (KCAT_W // LANES), axis=1) + pv
        m_scr[sub] = m_new

    def reset():
        m_scr[...] = jnp.full(m_scr.shape, NEG, f32)
        acc_scr[...] = jnp.zeros(acc_scr.shape, f32)

    @pl.when(g == 0)
    def _():
        reset()
        scores(q_sub(0), 0, s_a)

    def pair(t, c):
        scores(q_sub(1), t, s_b)
        consume(s_a, 0, t, False)
        scores(q_sub(0), t + 1, s_a)
        consume(s_b, 1, t, False)
        return c

    def quad(j, c):
        pair(2 * j, c)
        return pair(2 * j + 1, c)

    lax.fori_loop(0, g // 2, quad, 0)
    lax.fori_loop(2 * (g // 2), g, pair, 0)

    scores(q_sub(1), g, s_b)
    consume(s_a, 0, g, True)
    scores(qnext_ref[...], 0, s_a)
    consume(s_b, 1, g, True)
    for sub in range(2):
        acc = acc_scr[sub]
        out = acc[:, 0:KV_RANK] / acc[:, ONES_COL:ONES_COL + 1]
        o_ref[:, sub * tq:(sub + 1) * tq, :] = out.astype(bf16).reshape(N_HEADS, tq, KV_RANK)
    reset()


def _attn_prompt_call(q, kcat, *, tq):
    rows = kcat.shape[0]
    tk = 2 * tq
    nsub = rows // tq
    kern = functools.partial(_attn_prompt_kernel, tq=tq)
    return pl.pallas_call(
        kern,
        grid=(rows // tk,),
        in_specs=[pl.BlockSpec((N_HEADS, tk, KCAT_W), lambda g: (0, g, 0)),
                  pl.BlockSpec((N_HEADS, tq, KCAT_W), lambda g: (0, jnp.minimum(2 * g + 2, nsub - 1), 0)),
                  pl.BlockSpec((rows, KCAT_W), lambda g: (0, 0))],
        out_specs=pl.BlockSpec((N_HEADS, tk, KV_RANK), lambda g: (0, g, 0)),
        out_shape=jax.ShapeDtypeStruct((N_HEADS, rows, KV_RANK), bf16),
        scratch_shapes=[pltpu.VMEM((2, N_HEADS * tq, LANES), f32), pltpu.VMEM((2, N_HEADS * tq, KCAT_W), f32),
                        pltpu.VMEM((N_HEADS * tq, tk), f32), pltpu.VMEM((N_HEADS * tq, tk), f32)],
        compiler_params=pltpu.CompilerParams(dimension_semantics=("arbitrary",), vmem_limit_bytes=VMEM_LIMIT),
        name="attn_prompt",
    )(q, q, kcat)


def _attn_sample_kernel(pt_ref, q_ref, knew_ref, ckv_hbm, krt_hbm, o_ref, kbuf, rbuf, sem,
                        *, layer, n_batch, n_pages, n_new, n_chunks):
    b = pl.program_id(0)
    slot = b % 2

    def page_copies(bb, sl, p):
        page = pt_ref[bb, p]
        dst = pl.ds(p * PAGE_SIZE, PAGE_SIZE)
        return (pltpu.make_async_copy(ckv_hbm.at[layer, page], kbuf.at[sl, dst], sem.at[0, sl]),
                pltpu.make_async_copy(krt_hbm.at[layer, page], rbuf.at[sl, :, dst], sem.at[1, sl]))

    def issue(bb, sl, pages):
        for p in pages:
            for cpy in page_copies(bb, sl, p):
                cpy.start()

    def wait_all(bb, sl):
        for p in range(n_pages):
            for cpy in page_copies(bb, sl, p):
                cpy.wait()

    @pl.when(b == 0)
    def _():
        issue(0, 0, range(n_pages))

    @pl.when(b + 1 < n_batch)
    def _():
        issue(b + 1, 1 - slot, range(n_pages))

    wait_all(b, slot)

    q = q_ref[0]
    q_lat, q_rot = q[:, 0:KV_RANK], q[:, KV_RANK:KV_RANK + ROPE_DIM]
    kn = knew_ref[0]
    ck = n_pages * PAGE_SIZE // n_chunks

    def chunk_scores(c):
        kb = kbuf[slot, c * ck:(c + 1) * ck, :].astype(bf16)
        rb = rbuf[slot, :, c * ck:(c + 1) * ck].astype(bf16)
        return kb, _dot_nt(q_lat, kb) + _dot(q_rot, rb)

    def chunk_part(v, s):
        m = jnp.max(s, axis=1, keepdims=True)
        p = jnp.exp2(s - m)
        return m, jnp.sum(p, axis=1, keepdims=True), _dot(p.astype(bf16), v)

    parts = []
    cur = chunk_scores(0)
    for c in range(n_chunks):
        nxt = chunk_scores(c + 1) if c + 1 < n_chunks else None
        parts.append(chunk_part(*cur))
        cur = nxt
    sn = _dot_nt(q, kn)
    t_q = lax.broadcasted_iota(jnp.int32, sn.shape, 0) % n_new
    t_k = lax.broadcasted_iota(jnp.int32, sn.shape, 1)
    parts.append(chunk_part(kn[:, 0:KV_RANK], jnp.where(t_k <= t_q, sn, NEG)))

    m = functools.reduce(jnp.maximum, [pm for pm, _, _ in parts])
    l = jnp.zeros_like(m)
    o = jnp.zeros((q.shape[0], KV_RANK), f32)
    for pm, pl_, po in parts:
        w = jnp.exp2(pm - m)
        l = l + pl_ * w
        o = o + po * w
    o_ref[0] = (o / l).astype(bf16)


def _attn_sample_call(page_table, q, knew, cache_ckv, cache_krope_t, *, layer):
    n_batch, n_pages = page_table.shape
    past = n_pages * PAGE_SIZE
    rows_q = q.shape[1]
    kern = functools.partial(_attn_sample_kernel, layer=layer, n_batch=n_batch, n_pages=n_pages,
                             n_new=rows_q // N_HEADS, n_chunks=SAMPLE_CHUNKS)
    grid_spec = pltpu.PrefetchScalarGridSpec(
        num_scalar_prefetch=1,
        grid=(n_batch,),
        in_specs=[pl.BlockSpec((1, rows_q, KCAT_W), lambda b, pt: (b, 0, 0)),
                  pl.BlockSpec((1,) + knew.shape[1:], lambda b, pt: (b, 0, 0)),
                  pl.BlockSpec(memory_space=pl.ANY),
                  pl.BlockSpec(memory_space=pl.ANY)],
        out_specs=pl.BlockSpec((1, rows_q, KV_RANK), lambda b, pt: (b, 0, 0)),
        scratch_shapes=[pltpu.VMEM((2, past, KV_RANK), f32),
                        pltpu.VMEM((2, ROPE_DIM, past), f32),
                        pltpu.SemaphoreType.DMA((2, 2))],
    )
    return pl.pallas_call(
        kern,
        grid_spec=grid_spec,
        out_shape=jax.ShapeDtypeStruct((n_batch, rows_q, KV_RANK), bf16),
        compiler_params=pltpu.CompilerParams(dimension_semantics=("arbitrary",), vmem_limit_bytes=VMEM_LIMIT),
        name="attn_sample",
    )(page_table, q, knew, cache_ckv, cache_krope_t)


def _post_kernel(x_ref, o_ref, gatt_ref, mix_ref, wuv_ref, wout_ref, npost_ref, y_ref):
    oatt = jnp.concatenate(
        [_dot(jnp.concatenate([o_ref[2 * c], o_ref[2 * c + 1]], axis=1),
              wuv_ref[2 * c * KV_RANK:2 * (c + 1) * KV_RANK, 2 * c * V_DIM:2 * (c + 1) * V_DIM])
         for c in range(N_HEADS // 2)], axis=1)
    mixed = jnp.concatenate([(oatt * gatt_ref[...]).astype(bf16), mix_ref[...]], axis=1)
    out = _dot(mixed, wout_ref[...])
    y_ref[...] = x_ref[...] + _rms(out, npost_ref[...])


def _post_call(x, olat, gatt, mix, w, *, tm):
    rows = x.shape[0]
    rowblk = lambda width: pl.BlockSpec((tm, width), lambda i: (i, 0))
    full = lambda a: pl.BlockSpec(a.shape, lambda i: (0,) * a.ndim)
    return pl.pallas_call(
        _post_kernel,
        grid=(rows // tm,),
        in_specs=[rowblk(D_MODEL), pl.BlockSpec((N_HEADS, tm, KV_RANK), lambda i: (0, i, 0)),
                  rowblk(ATT_W), rowblk(POOL_W + CONV_W), full(w['w_uv']), full(w['w_out']), full(w['norm_post'])],
        out_specs=rowblk(D_MODEL),
        out_shape=jax.ShapeDtypeStruct((rows, D_MODEL), f32),
        compiler_params=pltpu.CompilerParams(dimension_semantics=("arbitrary",), vmem_limit_bytes=VMEM_LIMIT),
        name="post",
    )(x, olat, gatt, mix, w['w_uv'], w['w_out'], w['norm_post'])


def _swap_halves(wcols):
    half = wcols.shape[-1] // 2
    return jnp.concatenate([wcols[..., half:], wcols[..., :half]], axis=-1)


def _rope_slot(wcols):
    pad = jnp.zeros((wcols.shape[0], LANES - 2 * ROPE_DIM), wcols.dtype)
    return jnp.concatenate([wcols, _swap_halves(wcols), pad], axis=-1)


def _block_diag(blocks):
    g, a, b = blocks.shape
    eye = jnp.eye(g, dtype=blocks.dtype)
    return jnp.einsum('gab,gh->gahb', blocks, eye).reshape(g * a, g * b)


def _prep_layer(l, norm_pre, w_in, q_norm, w_uq, kv_norm, w_uk, w_uv, pool_w, pool_scale,
                conv_dw, conv_b, conv_ln_g, conv_ln_b, conv_pw, w_out, norm_post):
    wi = w_in[l]
    o_kr = Q_RANK + KV_RANK
    o_pool = o_kr + ROPE_DIM
    win = jnp.concatenate([wi[:, :o_kr], wi[:, o_pool:], _rope_slot(wi[:, o_kr:o_pool])], axis=1)
    wq = w_uq[l].reshape(Q_RANK, N_HEADS, NOPE_DIM + ROPE_DIM)
    wuq = jnp.concatenate(
        [wq[:, :, :NOPE_DIM].reshape(Q_RANK, N_HEADS * NOPE_DIM)]
        + [_rope_slot(wq[:, hh, NOPE_DIM:]) for hh in range(N_HEADS)], axis=1)
    row = lambda v: v.reshape(1, -1)
    return {
        'norm_pre': row(norm_pre[l]), 'w_in': win.astype(bf16), 'q_norm': row(q_norm[l]),
        'w_uq': wuq.astype(bf16), 'kv_norm': row(kv_norm[l]),
        'w_uk': _block_diag(jnp.transpose(w_uk[l], (1, 2, 0))).astype(bf16),
        'w_uv': _block_diag(jnp.transpose(w_uv[l], (1, 0, 2))).astype(bf16),
        'pool_w': _block_diag(pool_w[l]).astype(bf16), 'pool_scale': row(pool_scale[l]),
        'conv_dw': conv_dw[l], 'conv_b': row(conv_b[l]), 'conv_ln_g': row(conv_ln_g[l]),
        'conv_ln_b': row(conv_ln_b[l]), 'conv_pw': conv_pw[l].astype(bf16),
        'w_out': w_out[l].astype(bf16), 'norm_post': row(norm_post[l]),
    }


def _rope_table(pos):
    inv = ROPE_THETA ** (-jnp.arange(0, ROPE_DIM, 2, dtype=f32) / ROPE_DIM)
    ang = inv[:, None] * pos.astype(f32)[None, :]
    cos, sin = jnp.cos(ang), jnp.sin(ang)
    pad = jnp.zeros((LANES - 2 * ROPE_DIM, pos.shape[0]), f32)
    return jnp.concatenate([cos, cos, -sin, sin, pad], axis=0).T


def kernel(x_prompt, x_sample, cache_ckv, cache_krope, page_table, state_pool, state_conv, norm_pre, w_in,
           q_norm, w_uq, kv_norm, w_uk, w_uv, pool_w, pool_scale, conv_dw, conv_b, conv_ln_g, conv_ln_b,
           conv_pw, w_out, norm_post):
    bp, lp, _ = x_prompt.shape
    bs, ls, _ = x_sample.shape
    assert bp == 1
    depth = w_in.shape[0]
    past = page_table.shape[1] * PAGE_SIZE
    rows_s = bs * ls

    t1_p = _rope_table(jnp.arange(lp, dtype=jnp.int32))
    t1_s = _rope_table(past + jnp.repeat(jnp.arange(ls, dtype=jnp.int32), bs))
    hp = x_prompt.reshape(lp, D_MODEL)
    hs = jnp.transpose(x_sample, (1, 0, 2)).reshape(rows_s, D_MODEL)
    zero_hist = jnp.zeros((32, POOL_W), f32)
    cache_krope_t = jnp.swapaxes(cache_krope, 2, 3)

    def from_time_major(a, width):
        return jnp.transpose(a.reshape(ls, bs, width), (1, 0, 2))

    outs = {k: [] for k in ('ckv_p', 'kr_p', 'pool_p', 'conv_p', 'ckv_s', 'kr_s', 'pool_s', 'conv_s')}
    for l in range(depth):
        w = _prep_layer(l, norm_pre, w_in, q_norm, w_uq, kv_norm, w_uk, w_uv, pool_w, pool_scale,
                        conv_dw, conv_b, conv_ln_g, conv_ln_b, conv_pw, w_out, norm_post)
        q, kcat, ckv, kr, mix, gatt, ptail, ctail = _pre_call(
            hp, t1_p, zero_hist, zero_hist, w, tm=TM_PROMPT, stride=1, pos0=0)
        olat = _attn_prompt_call(q, kcat, tq=TQ_PROMPT)
        hp = _post_call(hp, olat, gatt, mix, w, tm=TM_PROMPT)
        outs['ckv_p'].append(ckv.reshape(bp, lp, KV_RANK))
        outs['kr_p'].append(kr.reshape(bp, lp, ROPE_DIM))
        outs['pool_p'].append(ptail[-POOL_HIST:].reshape(bp, POOL_HIST, POOL_W))
        outs['conv_p'].append(ctail[-CONV_HIST:].reshape(bp, CONV_HIST, CONV_W))
        hist_pool = jnp.transpose(state_pool[l], (1, 0, 2)).reshape(POOL_HIST * bs, POOL_W)
        hist_conv = jnp.transpose(state_conv[l], (1, 0, 2)).reshape(CONV_HIST * bs, CONV_W)
        q, kcat, ckv, kr, mix, gatt, ptail, ctail = _pre_call(
            hs, t1_s, hist_pool, hist_conv, w, tm=rows_s, stride=bs, pos0=past)
        q_b = jnp.transpose(q.reshape(N_HEADS, ls, bs, KCAT_W), (2, 0, 1, 3)).reshape(bs, N_HEADS * ls, KCAT_W)
        knew = jnp.pad(from_time_major(kcat, KCAT_W), ((0, 0), (0, 16 - ls), (0, 0)))
        o_b = _attn_sample_call(page_table, q_b, knew, cache_ckv, cache_krope_t, layer=l)
        olat = jnp.transpose(o_b.reshape(bs, N_HEADS, ls, KV_RANK), (1, 2, 0, 3)).reshape(N_HEADS, rows_s, KV_RANK)
        hs = _post_call(hs, olat, gatt, mix, w, tm=rows_s)
        outs['ckv_s'].append(from_time_major(ckv, KV_RANK))
        outs['kr_s'].append(from_time_major(kr, ROPE_DIM))
        outs['pool_s'].append(jnp.concatenate([state_pool[l][:, ls:], from_time_major(ptail, POOL_W)], axis=1))
        outs['conv_s'].append(jnp.concatenate([state_conv[l][:, ls:], from_time_major(ctail, CONV_W)], axis=1))

    y_prompt = hp.reshape(bp, lp, D_MODEL)
    y_sample = from_time_major(hs, D_MODEL)
    return (y_prompt, y_sample,
            jnp.stack(outs['ckv_p']), jnp.stack(outs['kr_p']), jnp.stack(outs['pool_p']), jnp.stack(outs['conv_p']),
            jnp.stack(outs['ckv_s']), jnp.stack(outs['kr_s']), jnp.stack(outs['pool_s']), jnp.stack(outs['conv_s']))
```

```python
import functools
import math

import jax
import jax.numpy as jnp
from jax import lax
from jax.experimental import pallas as pl
from jax.experimental.pallas import tpu as pltpu

D_MODEL = 1024
N_HEADS = 8
NOPE_DIM = 64
ROPE_DIM = 32
V_DIM = 64
Q_RANK = 256
KV_RANK = 128
ATT_W = N_HEADS * V_DIM
POOL_W = 256
POOL_GW = 64
POOL_HIST = 15
CONV_W = 256
CONV_K = 31
CONV_HIST = CONV_K - 1
PAGE_SIZE = 128
ATT_SCALE = (NOPE_DIM + ROPE_DIM) ** -0.5
Q_SCALE = ATT_SCALE * math.log2(math.e)
ROPE_THETA = 10000.0
EPS = 1e-6

LANES = 128
SUBLANES = 8
KCAT_W = 2 * LANES
ONES_COL = KV_RANK + ROPE_DIM
NEG = -1e30

C_Q, C_KV, C_POOL, C_CONV, C_GATE, C_KR, C_END = 0, 256, 384, 640, 1152, 2176, 2304
U_ROPE, U_END = N_HEADS * NOPE_DIM, N_HEADS * NOPE_DIM + N_HEADS * LANES

VMEM_LIMIT = 56 * 1024 * 1024

TM_PROMPT = 512
TQ_PROMPT = 256
SAMPLE_CHUNKS = 4

f32 = jnp.float32
bf16 = jnp.bfloat16


def _dot(a, b):
    return jnp.dot(a, b, preferred_element_type=f32)


def _dot_nt(a, b):
    return lax.dot_general(a, b, (((1,), (1,)), ((), ())), preferred_element_type=f32)


def _rms(x, g):
    return x * lax.rsqrt(jnp.mean(x * x, axis=-1, keepdims=True) + EPS) * g


def _sigmoid(x):
    return 1.0 / (1.0 + jnp.exp(-x))


def _align8(n):
    return (n + 7) // 8 * 8


def _pre_kernel(x_ref, t1_ref, hp_ref, hc_ref, npre_ref, win_ref, qn_ref, wuq_ref, kvn_ref, wuk_ref,
                poolw_ref, pscale_ref, dw_ref, db_ref, lng_ref, lnb_ref, pw_ref,
                q_ref, kcat_ref, ckv_ref, kr_ref, mix_ref, gatt_ref, ptail_ref, ctail_ref,
                pext, cext, buf2, buf4, buf8, ybuf, cshift, *, tm, stride, ph, ch, pos0, tail, carry):
    i = pl.program_id(0)

    @pl.when(i == 0)
    def _():
        pext[0:ph, :] = hp_ref[...]
        cext[0:ch, :] = hc_ref[...]

    h = _rms(x_ref[...], npre_ref[...]).astype(bf16)
    t1 = t1_ref[...]
    lane = lax.broadcasted_iota(jnp.int32, (tm, LANES), 1)
    n = ph + tm

    def rope(slot):
        pr = slot * t1
        return pr + pltpu.roll(pr, LANES - ROPE_DIM, 1)

    uc = _dot(h, win_ref[:, C_CONV:C_GATE])
    cext[ch:ch + tm, :] = uc[:, 0:CONV_W] * _sigmoid(uc[:, CONV_W:2 * CONV_W])
    pext[ph:n, :] = _dot(h, win_ref[:, C_POOL:C_CONV])
    shifted = stride % SUBLANES != 0
    if shifted:
        for j in range(1, SUBLANES):
            cshift[j - 1, SUBLANES:ch + tm, :] = cext[SUBLANES - j:ch + tm - j, :]
    rc = tm // 4

    def conv_rows(r0):
        acc = jnp.broadcast_to(db_ref[...], (rc, CONV_W))
        for k in range(CONV_K):
            back = (CONV_HIST - k) * stride
            j = back % SUBLANES if shifted else 0
            off = ch + r0 - (back - j)
            rows = cext[off:off + rc, :] if j == 0 else cshift[j - 1, off:off + rc, :]
            acc = acc + rows * dw_ref[k:k + 1, :]
        ybuf[r0:r0 + rc, :] = acc

    conv_rows(0)
    sgate = _dot(h, win_ref[:, C_GATE:C_KR])
    sgate = sgate * _sigmoid(sgate)
    gatt_ref[...] = sgate[:, 0:ATT_W]

    conv_rows(rc)
    cq = _rms(_dot(h, win_ref[:, C_Q:C_KV]), qn_ref[...]).astype(bf16)
    qz = _dot(cq, wuq_ref[...])
    qabs = _dot(qz[:, 0:U_ROPE].astype(bf16), wuk_ref[...])
    for hh in range(N_HEADS):
        q_ref[hh, :, 0:LANES] = (qabs[:, hh * LANES:(hh + 1) * LANES] * Q_SCALE).astype(bf16)
        qr = rope(qz[:, U_ROPE + hh * LANES:U_ROPE + (hh + 1) * LANES])
        q_ref[hh, :, LANES:KCAT_W] = jnp.where(lane < ROPE_DIM, qr * Q_SCALE, 0.0).astype(bf16)

    conv_rows(2 * rc)
    ckv =_rms(_dot(h, win_ref[:, C_KV:C_POOL]), kvn_ref[...])
    ckv_ref[...] = ckv
    kr = rope(_dot(h, win_ref[:, C_KR:C_END]))
    kr_ref[...] = kr[:, 0:ROPE_DIM]
    kcat_ref[:, 0:LANES] = ckv.astype(bf16)
    kcat_ref[:, LANES:KCAT_W] = jnp.where(
        lane < ROPE_DIM, kr, jnp.where(lane == ROPE_DIM, 1.0, 0.0)).astype(bf16)

    conv_rows(3 * rc)

    lo2 = _align8(stride)
    lo4 = _align8(lo2 + 2 * stride)
    lo8 = _align8(lo4 + 4 * stride)
    buf2[lo2:n, :] = pext[lo2:n, :] + pext[lo2 - stride:n - stride, :]
    buf4[lo4:n, :] = buf2[lo4:n, :] + buf2[lo4 - 2 * stride:n - 2 * stride, :]
    buf8[lo8:n, :] = buf4[lo8:n, :] + buf4[lo8 - 4 * stride:n - 4 * stride, :]
    s16 = buf8[ph:n, :] + buf8[ph - 8 * stride:n - 8 * stride, :]
    col = lax.broadcasted_iota(jnp.int32, (tm, POOL_W), 1)
    row = lax.broadcasted_iota(jnp.int32, (tm, POOL_W), 0)
    wsum = jnp.where(col < POOL_GW, buf2[ph:n, :],
                     jnp.where(col < 2 * POOL_GW, buf4[ph:n, :],
                               jnp.where(col < 3 * POOL_GW, buf8[ph:n, :], s16)))
    win = jnp.where(col < POOL_GW, 2, jnp.where(col < 2 * POOL_GW, 4, jnp.where(col < 3 * POOL_GW, 8, 16)))
    pos = pos0 + (i * tm + row) // stride
    cnt = jnp.minimum(pos + 1, win).astype(f32)
    d = wsum / cnt - pext[ph:n, :]
    ypool = _dot(d.astype(bf16), poolw_ref[...]) * pscale_ref[...]
    mix_ref[:, 0:POOL_W] = (ypool * sgate[:, ATT_W:ATT_W + POOL_W]).astype(bf16)

    y = ybuf[...]
    mu = jnp.mean(y, axis=-1, keepdims=True)
    yc = y - mu
    var = jnp.mean(yc * yc, axis=-1, keepdims=True)
    yn = yc * lax.rsqrt(var + EPS) * lng_ref[...] + lnb_ref[...]
    yconv = _dot((yn * _sigmoid(yn)).astype(bf16), pw_ref[...])
    mix_ref[:, POOL_W:POOL_W + CONV_W] = (yconv * sgate[:, ATT_W + POOL_W:]).astype(bf16)

    ptail_ref[...] = pext[n - tail:n, :]
    ctail_ref[...] = cext[ch + tm - tail:ch + tm, :]
    if carry:
        pext[0:ph, :] = pext[tm:tm + ph, :]
        cext[0:ch, :] = cext[tm:tm + ch, :]


def _pre_call(x, t1, hist_pool, hist_conv, w, *, tm, stride, pos0):
    rows = x.shape[0]
    ph, ch = hist_pool.shape[0], hist_conv.shape[0]
    nt = rows // tm
    tail = min(tm, _align8(CONV_HIST * stride))
    kern = functools.partial(_pre_kernel, tm=tm, stride=stride, ph=ph, ch=ch, pos0=pos0, tail=tail,
                             carry=nt > 1)
    rowblk = lambda width: pl.BlockSpec((tm, width), lambda i: (i, 0))
    full = lambda a: pl.BlockSpec(a.shape, lambda i: (0,) * a.ndim)
    weights = (w['norm_pre'], w['w_in'], w['q_norm'], w['w_uq'], w['kv_norm'], w['w_uk'], w['pool_w'],
               w['pool_scale'], w['conv_dw'], w['conv_b'], w['conv_ln_g'], w['conv_ln_b'], w['conv_pw'])
    return pl.pallas_call(
        kern,
        grid=(nt,),
        in_specs=[rowblk(D_MODEL), rowblk(LANES), full(hist_pool), full(hist_conv)] + [full(a) for a in weights],
        out_specs=[
            pl.BlockSpec((N_HEADS, tm, KCAT_W), lambda i: (0, i, 0)),
            rowblk(KCAT_W), rowblk(KV_RANK), rowblk(ROPE_DIM), rowblk(POOL_W + CONV_W), rowblk(ATT_W),
            pl.BlockSpec((tail, POOL_W), lambda i: (0, 0)),
            pl.BlockSpec((tail, CONV_W), lambda i: (0, 0)),
        ],
        out_shape=[
            jax.ShapeDtypeStruct((N_HEADS, rows, KCAT_W), bf16),
            jax.ShapeDtypeStruct((rows, KCAT_W), bf16),
            jax.ShapeDtypeStruct((rows, KV_RANK), f32),
            jax.ShapeDtypeStruct((rows, ROPE_DIM), f32),
            jax.ShapeDtypeStruct((rows, POOL_W + CONV_W), bf16),
            jax.ShapeDtypeStruct((rows, ATT_W), f32),
            jax.ShapeDtypeStruct((tail, POOL_W), f32),
            jax.ShapeDtypeStruct((tail, CONV_W), f32),
        ],
        scratch_shapes=[
            pltpu.VMEM((ph + tm, POOL_W), f32), pltpu.VMEM((ch + tm, CONV_W), f32),
            pltpu.VMEM((ph + tm, POOL_W), f32), pltpu.VMEM((ph + tm, POOL_W), f32),
            pltpu.VMEM((ph + tm, POOL_W), f32), pltpu.VMEM((tm, CONV_W), f32),
            pltpu.VMEM((SUBLANES - 1, ch + tm if stride % SUBLANES else SUBLANES, CONV_W), f32),
        ],
        compiler_params=pltpu.CompilerParams(dimension_semantics=("arbitrary",), vmem_limit_bytes=VMEM_LIMIT),
        name="pre",
    )(x, t1, hist_pool, hist_conv, *weights)


def _attn_prompt_kernel(q_ref, qnext_ref, k_ref, o_ref, m_scr, acc_scr, s_a, s_b, *, tq):
    g = pl.program_id(0)
    tk = 2 * tq
    m_rows = N_HEADS * tq

    def key_tile(t):
        return k_ref[pl.ds(pl.multiple_of(t * tk, tk), tk), :]

    def scores(q, t, dst):
        dst[...] = _dot_nt(q.reshape(m_rows, KCAT_W), key_tile(t))

    def q_sub(sub):
        return q_ref[:, sub * tq:(sub + 1) * tq, :]

    def consume(src, sub, t, masked):
        s = src[...]
        if masked:
            qoff = sub * tq + (lax.broadcasted_iota(jnp.int32, (m_rows, tk), 0) & (tq - 1))
            s = jnp.where(lax.broadcasted_iota(jnp.int32, (m_rows, tk), 1) <= qoff, s, NEG)
        m_prev = m_scr[sub]
        m_new = jnp.maximum(m_prev, jnp.max(s, axis=1, keepdims=True))
        alpha = jnp.exp2(m_prev - m_new)
        p = jnp.exp2(s - jnp.concatenate([m_new] * (tk // LANES), axis=1))
        pv = _dot(p.astype(bf16), key_tile(t))
        acc_scr[sub] = acc_scr[sub] * jnp.concatenate([alpha] * (KCAT_W // LANES), axis=1) + pv
        m_scr[sub] = m_new

    def reset():
        m_scr[...] = jnp.full(m_scr.shape, NEG, f32)
        acc_scr[...] = jnp.zeros(acc_scr.shape, f32)

    @pl.when(g == 0)
    def _():
        reset()
        scores(q_sub(0), 0, s_a)

    def pair(t, c):
        scores(q_sub(1), t, s_b)
        consume(s_a, 0, t, False)
        scores(q_sub(0), t + 1, s_a)
        consume(s_b, 1, t, False)
        return c

    def quad(j, c):
        pair(2 * j, c)
        return pair(2 * j + 1, c)

    lax.fori_loop(0, g // 2, quad, 0)
    lax.fori_loop(2 * (g // 2), g, pair, 0)

    scores(q_sub(1), g, s_b)
    consume(s_a, 0, g, True)
    scores(qnext_ref[...], 0, s_a)
    consume(s_b, 1, g, True)
    for sub in range(2):
        acc = acc_scr[sub]
        out = acc[:, 0:KV_RANK] / acc[:, ONES_COL:ONES_COL + 1]
        o_ref[:, sub * tq:(sub + 1) * tq, :] = out.astype(bf16).reshape(N_HEADS, tq, KV_RANK)
    reset()


def _attn_prompt_call(q, kcat, *, tq):
    rows = kcat.shape[0]
    tk = 2 * tq
    nsub = rows // tq
    kern = functools.partial(_attn_prompt_kernel, tq=tq)
    return pl.pallas_call(
        kern,
        grid=(rows // tk,),
        in_specs=[pl.BlockSpec((N_HEADS, tk, KCAT_W), lambda g: (0, g, 0)),
                  pl.BlockSpec((N_HEADS, tq, KCAT_W), lambda g: (0, jnp.minimum(2 * g + 2, nsub - 1), 0)),
                  pl.BlockSpec((rows, KCAT_W), lambda g: (0, 0))],
        out_specs=pl.BlockSpec((N_HEADS, tk, KV_RANK), lambda g: (0, g, 0)),
        out_shape=jax.ShapeDtypeStruct((N_HEADS, rows, KV_RANK), bf16),
        scratch_shapes=[pltpu.VMEM((2, N_HEADS * tq, LANES), f32), pltpu.VMEM((2, N_HEADS * tq, KCAT_W), f32),
                        pltpu.VMEM((N_HEADS * tq, tk), f32), pltpu.VMEM((N_HEADS * tq, tk), f32)],
        compiler_params=pltpu.CompilerParams(dimension_semantics=("arbitrary",), vmem_limit_bytes=VMEM_LIMIT),
        name="attn_prompt",
    )(q, q, kcat)


def _attn_sample_kernel(pt_ref, q_ref, knew_ref, ckv_hbm, krt_hbm, o_ref, kbuf, rbuf, sem,
                        *, layer, n_batch, n_pages, n_new, n_chunks):
    b = pl.program_id(0)
    slot = b % 2

    def page_copies(bb, sl, p):
        page = pt_ref[bb, p]
        dst = pl.ds(p * PAGE_SIZE, PAGE_SIZE)
        return (pltpu.make_async_copy(ckv_hbm.at[layer, page], kbuf.at[sl, dst], sem.at[0, sl]),
                pltpu.make_async_copy(krt_hbm.at[layer, page], rbuf.at[sl, :, dst], sem.at[1, sl]))

    def issue(bb, sl, pages):
        for p in pages:
            for priority, cpy in enumerate(page_copies(bb, sl, p)):
                cpy.start(priority=priority)

    def wait_all(bb, sl):
        for p in range(n_pages):
            for cpy in page_copies(bb, sl, p):
                cpy.wait()

    @pl.when(b == 0)
    def _():
        issue(0, 0, range(n_pages))

    @pl.when(b + 1 < n_batch)
    def _():
        issue(b + 1, 1 - slot, range(n_pages))

    wait_all(b, slot)

    q = q_ref[0]
    q_lat, q_rot = q[:, 0:KV_RANK], q[:, KV_RANK:KV_RANK + ROPE_DIM]
    kn = knew_ref[0]
    ck = n_pages * PAGE_SIZE // n_chunks

    def chunk_scores(c):
        kb = kbuf[slot, c * ck:(c + 1) * ck, :].astype(bf16)
        rb = rbuf[slot, :, c * ck:(c + 1) * ck].astype(bf16)
        return kb, _dot_nt(q_lat, kb) + _dot(q_rot, rb)

    def chunk_part(v, s):
        m = jnp.max(s, axis=1, keepdims=True)
        p = jnp.exp2(s - m)
        return m, jnp.sum(p, axis=1, keepdims=True), _dot(p.astype(bf16), v)

    parts = []
    cur = chunk_scores(0)
    for c in range(n_chunks):
        nxt = chunk_scores(c + 1) if c + 1 < n_chunks else None
        parts.append(chunk_part(*cur))
        cur = nxt
    sn = _dot_nt(q, kn)
    t_q = lax.broadcasted_iota(jnp.int32, sn.shape, 0) % n_new
    t_k = lax.broadcasted_iota(jnp.int32, sn.shape, 1)
    parts.append(chunk_part(kn[:, 0:KV_RANK], jnp.where(t_k <= t_q, sn, NEG)))

    m = functools.reduce(jnp.maximum, [pm for pm, _, _ in parts])
    l = jnp.zeros_like(m)
    o = jnp.zeros((q.shape[0], KV_RANK), f32)
    for pm, pl_, po in parts:
        w = jnp.exp2(pm - m)
        l = l + pl_ * w
        o = o + po * w
    o_ref[0] = (o / l).astype(bf16)


def _attn_sample_call(page_table, q, knew, cache_ckv, cache_krope_t, *, layer):
    n_batch, n_pages = page_table.shape
    past = n_pages * PAGE_SIZE
    rows_q = q.shape[1]
    kern = functools.partial(_attn_sample_kernel, layer=layer, n_batch=n_batch, n_pages=n_pages,
                             n_new=rows_q // N_HEADS, n_chunks=SAMPLE_CHUNKS)
    grid_spec = pltpu.PrefetchScalarGridSpec(
        num_scalar_prefetch=1,
        grid=(n_batch,),
        in_specs=[pl.BlockSpec((1, rows_q, KCAT_W), lambda b, pt: (b, 0, 0)),
                  pl.BlockSpec((1,) + knew.shape[1:], lambda b, pt: (b, 0, 0)),
                  pl.BlockSpec(memory_space=pl.ANY),
                  pl.BlockSpec(memory_space=pl.ANY)],
        out_specs=pl.BlockSpec((1, rows_q, KV_RANK), lambda b, pt: (b, 0, 0)),
        scratch_shapes=[pltpu.VMEM((2, past, KV_RANK), f32),
                        pltpu.VMEM((2, ROPE_DIM, past), f32),
                        pltpu.SemaphoreType.DMA((2, 2))],
    )
    return pl.pallas_call(
        kern,
        grid_spec=grid_spec,
        out_shape=jax.ShapeDtypeStruct((n_batch, rows_q, KV_RANK), bf16),
        compiler_params=pltpu.CompilerParams(dimension_semantics=("arbitrary",), vmem_limit_bytes=VMEM_LIMIT),
        name="attn_sample",
    )(page_table, q, knew, cache_ckv, cache_krope_t)


def _post_kernel(x_ref, o_ref, gatt_ref, mix_ref, wuv_ref, wout_ref, npost_ref, y_ref):
    oatt = jnp.concatenate(
        [_dot(jnp.concatenate([o_ref[2 * c], o_ref[2 * c + 1]], axis=1),
              wuv_ref[2 * c * KV_RANK:2 * (c + 1) * KV_RANK, 2 * c * V_DIM:2 * (c + 1) * V_DIM])
         for c in range(N_HEADS // 2)], axis=1)
    mixed = jnp.concatenate([(oatt * gatt_ref[...]).astype(bf16), mix_ref[...]], axis=1)
    out = _dot(mixed, wout_ref[...])
    y_ref[...] = x_ref[...] + _rms(out, npost_ref[...])


def _post_call(x, olat, gatt, mix, w, *, tm):
    rows = x.shape[0]
    rowblk = lambda width: pl.BlockSpec((tm, width), lambda i: (i, 0))
    full = lambda a: pl.BlockSpec(a.shape, lambda i: (0,) * a.ndim)
    return pl.pallas_call(
        _post_kernel,
        grid=(rows // tm,),
        in_specs=[rowblk(D_MODEL), pl.BlockSpec((N_HEADS, tm, KV_RANK), lambda i: (0, i, 0)),
                  rowblk(ATT_W), rowblk(POOL_W + CONV_W), full(w['w_uv']), full(w['w_out']), full(w['norm_post'])],
        out_specs=rowblk(D_MODEL),
        out_shape=jax.ShapeDtypeStruct((rows, D_MODEL), f32),
        compiler_params=pltpu.CompilerParams(dimension_semantics=("arbitrary",), vmem_limit_bytes=VMEM_LIMIT),
        name="post",
    )(x, olat, gatt, mix, w['w_uv'], w['w_out'], w['norm_post'])


def _swap_halves(wcols):
    half = wcols.shape[-1] // 2
    return jnp.concatenate([wcols[..., half:], wcols[..., :half]], axis=-1)


def _rope_slot(wcols):
    pad = jnp.zeros((wcols.shape[0], LANES - 2 * ROPE_DIM), wcols.dtype)
    return jnp.concatenate([wcols, _swap_halves(wcols), pad], axis=-1)


def _block_diag(blocks):
    g, a, b = blocks.shape
    eye = jnp.eye(g, dtype=blocks.dtype)
    return jnp.einsum('gab,gh->gahb', blocks, eye).reshape(g * a, g * b)


def _prep_layer(l, norm_pre, w_in, q_norm, w_uq, kv_norm, w_uk, w_uv, pool_w, pool_scale,
                conv_dw, conv_b, conv_ln_g, conv_ln_b, conv_pw, w_out, norm_post):
    wi = w_in[l]
    o_kr = Q_RANK + KV_RANK
    o_pool = o_kr + ROPE_DIM
    win = jnp.concatenate([wi[:, :o_kr], wi[:, o_pool:], _rope_slot(wi[:, o_kr:o_pool])], axis=1)
    wq = w_uq[l].reshape(Q_RANK, N_HEADS, NOPE_DIM + ROPE_DIM)
    wuq = jnp.concatenate(
        [wq[:, :, :NOPE_DIM].reshape(Q_RANK, N_HEADS * NOPE_DIM)]
        + [_rope_slot(wq[:, hh, NOPE_DIM:]) for hh in range(N_HEADS)], axis=1)
    row = lambda v: v.reshape(1, -1)
    return {
        'norm_pre': row(norm_pre[l]), 'w_in': win.astype(bf16), 'q_norm': row(q_norm[l]),
        'w_uq': wuq.astype(bf16), 'kv_norm': row(kv_norm[l]),
        'w_uk': _block_diag(jnp.transpose(w_uk[l], (1, 2, 0))).astype(bf16),
        'w_uv': _block_diag(jnp.transpose(w_uv[l], (1, 0, 2))).astype(bf16),
        'pool_w': _block_diag(pool_w[l]).astype(bf16), 'pool_scale': row(pool_scale[l]),
        'conv_dw': conv_dw[l], 'conv_b': row(conv_b[l]), 'conv_ln_g': row(conv_ln_g[l]),
        'conv_ln_b': row(conv_ln_b[l]), 'conv_pw': conv_pw[l].astype(bf16),
        'w_out': w_out[l].astype(bf16), 'norm_post': row(norm_post[l]),
    }


def _rope_table(pos):
    inv = ROPE_THETA ** (-jnp.arange(0, ROPE_DIM, 2, dtype=f32) / ROPE_DIM)
    ang = inv[:, None] * pos.astype(f32)[None, :]
    cos, sin = jnp.cos(ang), jnp.sin(ang)
    pad = jnp.zeros((LANES - 2 * ROPE_DIM, pos.shape[0]), f32)
    return jnp.concatenate([cos, cos, -sin, sin, pad], axis=0).T


def kernel(x_prompt, x_sample, cache_ckv, cache_krope, page_table, state_pool, state_conv, norm_pre, w_in,
           q_norm, w_uq, kv_norm, w_uk, w_uv, pool_w, pool_scale, conv_dw, conv_b, conv_ln_g, conv_ln_b,
           conv_pw, w_out, norm_post):
    bp, lp, _ = x_prompt.shape
    bs, ls, _ = x_sample.shape
    assert bp == 1
    depth = w_in.shape[0]
    past = page_table.shape[1] * PAGE_SIZE
    rows_s = bs * ls

    t1_p = _rope_table(jnp.arange(lp, dtype=jnp.int32))
    t1_s = _rope_table(past + jnp.repeat(jnp.arange(ls, dtype=jnp.int32), bs))
    hp = x_prompt.reshape(lp, D_MODEL)
    hs = jnp.transpose(x_sample, (1, 0, 2)).reshape(rows_s, D_MODEL)
    zero_hist = jnp.zeros((32, POOL_W), f32)
    cache_krope_t = jnp.swapaxes(cache_krope, 2, 3)

    def from_time_major(a, width):
        return jnp.transpose(a.reshape(ls, bs, width), (1, 0, 2))

    outs = {k: [] for k in ('ckv_p', 'kr_p', 'pool_p', 'conv_p', 'ckv_s', 'kr_s', 'pool_s', 'conv_s')}
    for l in range(depth):
        w = _prep_layer(l, norm_pre, w_in, q_norm, w_uq, kv_norm, w_uk, w_uv, pool_w, pool_scale,
                        conv_dw, conv_b, conv_ln_g, conv_ln_b, conv_pw, w_out, norm_post)
        q, kcat, ckv, kr, mix, gatt, ptail, ctail = _pre_call(
            hp, t1_p, zero_hist, zero_hist, w, tm=TM_PROMPT, stride=1, pos0=0)
        olat = _attn_prompt_call(q, kcat, tq=TQ_PROMPT)
        hp = _post_call(hp, olat, gatt, mix, w, tm=TM_PROMPT)
        outs['ckv_p'].append(ckv.reshape(bp, lp, KV_RANK))
        outs['kr_p'].append(kr.reshape(bp, lp, ROPE_DIM))
        outs['pool_p'].append(ptail[-POOL_HIST:].reshape(bp, POOL_HIST, POOL_W))
        outs['conv_p'].append(ctail[-CONV_HIST:].reshape(bp, CONV_HIST, CONV_W))
        hist_pool = jnp.transpose(state_pool[l], (1, 0, 2)).reshape(POOL_HIST * bs, POOL_W)
        hist_conv = jnp.transpose(state_conv[l], (1, 0, 2)).reshape(CONV_HIST * bs, CONV_W)
        q, kcat, ckv, kr, mix, gatt, ptail, ctail = _pre_call(
            hs, t1_s, hist_pool, hist_conv, w, tm=rows_s, stride=bs, pos0=past)
        q_b = jnp.transpose(q.reshape(N_HEADS, ls, bs, KCAT_W), (2, 0, 1, 3)).reshape(bs, N_HEADS * ls, KCAT_W)
        knew = jnp.pad(from_time_major(kcat, KCAT_W), ((0, 0), (0, 16 - ls), (0, 0)))
        o_b = _attn_sample_call(page_table, q_b, knew, cache_ckv, cache_krope_t, layer=l)
        olat = jnp.transpose(o_b.reshape(bs, N_HEADS, ls, KV_RANK), (1, 2, 0, 3)).reshape(N_HEADS, rows_s, KV_RANK)
        hs = _post_call(hs, olat, gatt, mix, w, tm=rows_s)
        outs['ckv_s'].append(from_time_major(ckv, KV_RANK))
        outs['kr_s'].append(from_time_major(kr, ROPE_DIM))
        outs['pool_s'].append(jnp.concatenate([state_pool[l][:, ls:], from_time_major(ptail, POOL_W)], axis=1))
        outs['conv_s'].append(jnp.concatenate([state_conv[l][:, ls:], from_time_major(ctail, CONV_W)], axis=1))

    y_prompt = hp.reshape(bp, lp, D_MODEL)
    y_sample = from_time_major(hs, D_MODEL)
    return (y_prompt, y_sample,
            jnp.stack(outs['ckv_p']), jnp.stack(outs['kr_p']), jnp.stack(outs['pool_p']), jnp.stack(outs['conv_p']),
            jnp.stack(outs['ckv_s']), jnp.stack(outs['kr_s']), jnp.stack(outs['pool_s']), jnp.stack(outs['conv_s']))
```
